```python
import math
import jax, jax.numpy as jnp
from jax import lax
import numpy as np

D_MODEL = 2048
BATCH = 8
SEQ = 2048
DEPTH = 2

N_MIXERS = 2
N_RET_LAYERS = (DEPTH + 1) // 2
N_MOBA_LAYERS = DEPTH // 2
DEEPNORM_ALPHA = (2.0 * DEPTH) ** 0.25
DEEPNORM_BETA = (8.0 * DEPTH) ** -0.25
LN_EPS = 1e-5
ADA_SCALE = 0.1

RET_HEADS = 8
RET_DK = D_MODEL // RET_HEADS
RET_DV = 2 * D_MODEL // RET_HEADS
RET_QK = RET_HEADS * RET_DK
RET_V = RET_HEADS * RET_DV
RET_GATE = RET_V
RET_IN = 2 * RET_QK + RET_V + RET_GATE
RET_CHUNK = 128
RET_THETA = 10000.0
RET_GN_EPS = 1e-5

MOBA_HEADS = 16
MOBA_DH = D_MODEL // MOBA_HEADS
MOBA_W = MOBA_HEADS * MOBA_DH
MOBA_IN = 4 * MOBA_W
MOBA_BLOCK = 256
MOBA_TOPK = 3
MOBA_QCHUNK = 8
ROPE_THETA = 500000.0
ROPE_DIMS = MOBA_DH // 4
NEG = -1e30

kernel_name = "hybrid_retention_moba_adaln_deepnorm"


def _rope(x, pos, n_rot, theta):
    half = n_rot // 2
    inv = theta ** (-jnp.arange(half, dtype=jnp.float32) * 2.0 / n_rot)
    ang = pos.astype(jnp.float32)[..., None] * inv
    cos = jnp.cos(ang)[:, :, None, :].astype(x.dtype)
    sin = jnp.sin(ang)[:, :, None, :].astype(x.dtype)
    x1 = x[..., :half]
    x2 = x[..., half:n_rot]
    return jnp.concatenate([x1 * cos - x2 * sin, x2 * cos + x1 * sin, x[..., n_rot:]], axis=-1)


def _layernorm(x, g, b):
    xf = x.astype(jnp.float32)
    mu = jnp.mean(xf, axis=-1, keepdims=True)
    var = jnp.mean(jnp.square(xf - mu), axis=-1, keepdims=True)
    y = (xf - mu) * lax.rsqrt(var + LN_EPS)
    return (y * g.astype(jnp.float32) + b.astype(jnp.float32)).astype(x.dtype)


def _retention(h, w_in, w_out, positions):
    B, S, _ = h.shape
    proj = h @ w_in
    q = proj[..., :RET_QK].reshape(B, S, RET_HEADS, RET_DK)
    k = proj[..., RET_QK:2 * RET_QK].reshape(B, S, RET_HEADS, RET_DK)
    v = proj[..., 2 * RET_QK:2 * RET_QK + RET_V].reshape(B, S, RET_HEADS, RET_DV)
    g = proj[..., 2 * RET_QK + RET_V:]
    q = _rope(q, positions, RET_DK, RET_THETA)
    k = _rope(k, positions, RET_DK, RET_THETA) * (RET_DK ** -0.5)

    n_chunks = S // RET_CHUNK
    def to_chunks(t):
        return t.reshape(B, n_chunks, RET_CHUNK, RET_HEADS, t.shape[-1]).transpose(1, 0, 3, 2, 4)
    qc, kc, vc = to_chunks(q), to_chunks(k), to_chunks(v)

    log_gamma = jnp.log(1.0 - 2.0 ** (-5.0 - jnp.arange(RET_HEADS, dtype=jnp.float32)))
    n = jnp.arange(RET_CHUNK, dtype=jnp.float32)
    diff = n[:, None] - n[None, :]
    inner_decay = jnp.where(diff >= 0, jnp.exp(jnp.maximum(diff, 0.0) * log_gamma[:, None, None]), 0.0).astype(h.dtype)
    q_decay = jnp.exp((n + 1.0) * log_gamma[:, None]).astype(h.dtype)
    k_decay = jnp.exp((RET_CHUNK - 1.0 - n) * log_gamma[:, None]).astype(h.dtype)
    chunk_decay = jnp.exp(RET_CHUNK * log_gamma).astype(h.dtype)

    def step(state, xs):
        qi, ki, vi = xs
        scores = jnp.einsum('bhnd,bhmd->bhnm', qi, ki) * inner_decay
        inner = jnp.einsum('bhnm,bhmv->bhnv', scores, vi)
        cross = jnp.einsum('bhnd,bhdv->bhnv', qi, state) * q_decay[None, :, :, None]
        new_state = state * chunk_decay[None, :, None, None] + jnp.einsum(
            'bhmd,bhmv->bhdv', ki * k_decay[None, :, :, None], vi)
        return new_state, inner + cross

    state0 = jnp.zeros((B, RET_HEADS, RET_DK, RET_DV), dtype=qc.dtype)
    _, o = lax.scan(step, state0, (qc, kc, vc))
    o = o.transpose(1, 0, 3, 2, 4).reshape(B, S, RET_HEADS, RET_DV)
    of = o.astype(jnp.float32)
    mu = jnp.mean(of, axis=-1, keepdims=True)
    var = jnp.mean(jnp.square(of - mu), axis=-1, keepdims=True)
    o = ((of - mu) * lax.rsqrt(var + RET_GN_EPS)).astype(h.dtype).reshape(B, S, RET_V)
    return (o * jax.nn.silu(g)) @ w_out


def _moba(h, w_in, w_out, positions):
    B, S, _ = h.shape
    proj = h @ w_in
    q = proj[..., :MOBA_W].reshape(B, S, MOBA_HEADS, MOBA_DH)
    k = proj[..., MOBA_W:2 * MOBA_W].reshape(B, S, MOBA_HEADS, MOBA_DH)
    v = proj[..., 2 * MOBA_W:3 * MOBA_W].reshape(B, S, MOBA_HEADS, MOBA_DH)
    g = proj[..., 3 * MOBA_W:]
    q = _rope(q, positions, ROPE_DIMS, ROPE_THETA).transpose(0, 2, 1, 3)
    k = _rope(k, positions, ROPE_DIMS, ROPE_THETA).transpose(0, 2, 1, 3)
    v = v.transpose(0, 2, 1, 3)

    n_blocks = -(-S // MOBA_BLOCK)
    pad = n_blocks * MOBA_BLOCK - S
    kb = jnp.pad(k, ((0, 0), (0, 0), (0, pad), (0, 0))).reshape(B, MOBA_HEADS, n_blocks, MOBA_BLOCK, MOBA_DH)
    vb = jnp.pad(v, ((0, 0), (0, 0), (0, pad), (0, 0))).reshape(B, MOBA_HEADS, n_blocks, MOBA_BLOCK, MOBA_DH)
    k_mean = jnp.mean(kb, axis=3)

    t = jnp.arange(S)
    blk_t = t // MOBA_BLOCK
    gate = jnp.einsum('bhtd,bhnd->bhtn', q, k_mean).astype(jnp.float32)
    past = jnp.arange(n_blocks)[None, :] < blk_t[:, None]
    gate = jnp.where(past, gate, NEG)
    topk = min(MOBA_TOPK, n_blocks)
    _, top_idx = lax.top_k(gate, topk)
    valid = top_idx < blk_t[None, None, :, None]

    n_q = S // MOBA_QCHUNK
    def qchunks(a):
        a = a.reshape(B, MOBA_HEADS, n_q, MOBA_QCHUNK, *a.shape[3:])
        return jnp.moveaxis(a, 2, 0)
    qs, idxs, valids = qchunks(q), qchunks(top_idx), qchunks(valid)
    scale = MOBA_DH ** -0.5
    bi = jnp.arange(B)[:, None, None, None]
    hi = jnp.arange(MOBA_HEADS)[None, :, None, None]

    def chunk_fn(args):
        ci, qc, idxc, validc = args
        tq = ci * MOBA_QCHUNK + jnp.arange(MOBA_QCHUNK)
        own = (ci * MOBA_QCHUNK) // MOBA_BLOCK
        k_own = lax.dynamic_index_in_dim(kb, own, axis=2, keepdims=False)
        v_own = lax.dynamic_index_in_dim(vb, own, axis=2, keepdims=False)
        kpos_own = own * MOBA_BLOCK + jnp.arange(MOBA_BLOCK)
        s_own = jnp.einsum('bhqd,bhkd->bhqk', qc, k_own).astype(jnp.float32) * scale
        s_own = jnp.where(kpos_own[None, :] <= tq[:, None], s_own, NEG)
        kg = kb[bi, hi, idxc]
        vg = vb[bi, hi, idxc]
        s_sel = jnp.einsum('bhqd,bhqnkd->bhqnk', qc, kg).astype(jnp.float32) * scale
        s_sel = jnp.where(validc[..., None], s_sel, NEG)
        logits = jnp.concatenate([s_sel.reshape(*s_sel.shape[:3], -1), s_own], axis=-1)
        p = jax.nn.softmax(logits, axis=-1).astype(qc.dtype)
        n_sel = topk * MOBA_BLOCK
        p_sel = p[..., :n_sel].reshape(s_sel.shape)
        p_own = p[..., n_sel:]
        return (jnp.einsum('bhqnk,bhqnkd->bhqd', p_sel, vg)
                + jnp.einsum('bhqk,bhkd->bhqd', p_own, v_own))

    o = lax.map(chunk_fn, (jnp.arange(n_q), qs, idxs, valids))
    o = o.transpose(1, 0, 3, 2, 4).reshape(B, S, MOBA_W)
    return (o * jax.nn.silu(g)) @ w_out


def setup_inputs(seed: int = 0) -> dict:
    key = jax.random.key(seed)
    ks = jax.random.split(key, 16)
    D = D_MODEL
    x = jax.random.normal(ks[0], (BATCH, SEQ, D), jnp.float32)
    c = jax.random.normal(ks[1], (BATCH, D), jnp.float32)
    offs = jax.random.randint(ks[2], (BATCH, 1), 0, 4096, dtype=jnp.int32)
    positions = (offs + jnp.arange(SEQ, dtype=jnp.int32)[None, :]).astype(jnp.int32)

    ret_w_in = jax.random.normal(ks[3], (N_RET_LAYERS, D, RET_IN), jnp.float32) * D ** -0.5
    col = jnp.arange(RET_IN)
    v_cols = (col >= 2 * RET_QK) & (col < 2 * RET_QK + RET_V)
    ret_w_in = ret_w_in * jnp.where(v_cols, DEEPNORM_BETA, 1.0).astype(jnp.float32)
    ret_w_out = jax.random.normal(ks[4], (N_RET_LAYERS, RET_V, D), jnp.float32) * (RET_V ** -0.5 * DEEPNORM_BETA)

    moba_w_in = jax.random.normal(ks[5], (N_MOBA_LAYERS, D, MOBA_IN), jnp.float32) * D ** -0.5
    mcol = jnp.arange(MOBA_IN)
    mv_cols = (mcol >= 2 * MOBA_W) & (mcol < 3 * MOBA_W)
    moba_w_in = moba_w_in * jnp.where(mv_cols, DEEPNORM_BETA, 1.0).astype(jnp.float32)
    moba_w_out = jax.random.normal(ks[6], (N_MOBA_LAYERS, MOBA_W, D), jnp.float32) * (MOBA_W ** -0.5 * DEEPNORM_BETA)

    w_ada = jax.random.normal(ks[7], (DEPTH, D, 3 * D), jnp.float32) * (ADA_SCALE * D ** -0.5)
    b_ada = 0.02 * jax.random.normal(ks[8], (DEPTH, 3 * D), jnp.float32)
    b_ada = b_ada + jnp.concatenate([jnp.zeros((2 * D,)), jnp.ones((D,))]).astype(jnp.float32)[None, :]
    ln_g = 1.0 + 0.02 * jax.random.normal(ks[9], (DEPTH, D), jnp.float32)
    ln_b = 0.02 * jax.random.normal(ks[10], (DEPTH, D), jnp.float32)
    return {"x": x, "c": c, "positions": positions,
            "ret_w_in": ret_w_in, "ret_w_out": ret_w_out,
            "moba_w_in": moba_w_in, "moba_w_out": moba_w_out,
            "w_ada": w_ada, "b_ada": b_ada, "ln_g": ln_g, "ln_b": ln_b}


def reference(x, c, positions, ret_w_in, ret_w_out, moba_w_in, moba_w_out, w_ada, b_ada, ln_g, ln_b):
    D = x.shape[-1]
    for layer in range(DEPTH):
        mod = c @ w_ada[layer] + b_ada[layer]
        shift = mod[:, None, :D]
        scale = mod[:, None, D:2 * D]
        gate = mod[:, None, 2 * D:]
        h = x * (1.0 + scale) + shift
        if layer % N_MIXERS == 0:
            y = _retention(h, ret_w_in[layer // N_MIXERS], ret_w_out[layer // N_MIXERS], positions)
        else:
            y = _moba(h, moba_w_in[layer // N_MIXERS], moba_w_out[layer // N_MIXERS], positions)
        x = _layernorm(DEEPNORM_ALPHA * x + gate * y, ln_g[layer], ln_b[layer])
    return x
```

```python
import functools

import jax
import jax.numpy as jnp
from jax import lax
from jax.experimental import pallas as pl
from jax.experimental.pallas import tpu as pltpu

F32 = jnp.float32
BF16 = jnp.bfloat16

DEPTH = 2
N_MIXERS = 2
DEEPNORM_ALPHA = (2.0 * DEPTH) ** 0.25
LN_EPS = 1e-5

RET_HEADS = 8
RET_CHUNK = 256
RET_THETA = 10000.0
RET_GN_EPS = 1e-5

MOBA_HEADS = 16
MOBA_BLOCK = 256
MOBA_TOPK = 3
ROPE_THETA = 500000.0
NEG = -1e30

LANES = 128
VMEM_LIMIT = 56 * 1024 * 1024


def _params(sem):
    return pltpu.CompilerParams(dimension_semantics=sem, vmem_limit_bytes=VMEM_LIMIT)


def _ada_kernel(c_ref, w_ref, b_ref, o_ref):
    acc = jnp.dot(c_ref[...].astype(BF16), w_ref[...].astype(BF16),
                  preferred_element_type=F32)
    o_ref[...] = acc + b_ref[...]


def _ada(c, w_ada, b_ada, tn=512):
    depth, d, n = w_ada.shape
    b = c.shape[0]
    return pl.pallas_call(
        _ada_kernel,
        grid=(depth, n // tn),
        in_specs=[
            pl.BlockSpec((b, d), lambda l, j: (0, 0)),
            pl.BlockSpec((None, d, tn), lambda l, j: (l, 0, j)),
            pl.BlockSpec((None, 1, tn), lambda l, j: (l, 0, j)),
        ],
        out_specs=pl.BlockSpec((None, b, tn), lambda l, j: (l, 0, j)),
        out_shape=jax.ShapeDtypeStruct((depth, b, n), F32),
        compiler_params=_params(("parallel", "parallel")),
        name="ada_mod",
    )(c, w_ada, b_ada.reshape(depth, 1, n))


def _rope_tables_kernel(pos_ref, inv_r_ref, inv_m_ref, cos_r, sin_r, mc, ms1, ms2, *, half_m):
    pos = pos_ref[0].astype(F32)
    ang = pos * inv_r_ref[...]
    cos_r[0] = jnp.cos(ang)
    sin_r[0] = jnp.sin(ang)
    angm = pos * inv_m_ref[...]
    cm = jnp.cos(angm)
    sm = jnp.sin(angm)
    lane = lax.broadcasted_iota(jnp.int32, angm.shape, 1)
    mc[0] = jnp.where(lane < 2 * half_m, cm, 1.0)
    ms1[0] = jnp.where(lane < half_m, 0.0, jnp.where(lane < 2 * half_m, sm, 0.0))
    ms2[0] = jnp.where(lane < half_m, -sm, 0.0)


def _rope_tables(positions, ret_dk, moba_rot, ts=512):
    b, s = positions.shape
    half_r = ret_dk // 2
    half_m = moba_rot // 2
    assert half_r == LANES and LANES % half_m == 0
    inv_r = RET_THETA ** (-jnp.arange(half_r, dtype=F32) * 2.0 / ret_dk)
    inv_m = ROPE_THETA ** (-jnp.arange(half_m, dtype=F32) * 2.0 / moba_rot)
    inv_m = jnp.tile(inv_m, LANES // half_m)
    tab = jax.ShapeDtypeStruct((b, s, LANES), F32)
    spec = pl.BlockSpec((1, ts, LANES), lambda i, j: (i, j, 0))
    row = pl.BlockSpec((1, LANES), lambda i, j: (0, 0))
    return pl.pallas_call(
        functools.partial(_rope_tables_kernel, half_m=half_m),
        grid=(b, s // ts),
        in_specs=[pl.BlockSpec((1, ts, 1), lambda i, j: (i, j, 0)), row, row],
        out_specs=[spec] * 5,
        out_shape=[tab] * 5,
        compiler_params=_params(("parallel", "parallel")),
        name="rope_tables",
    )(positions.reshape(b, s, 1), inv_r.reshape(1, LANES), inv_m.reshape(1, LANES))


def _inproj_kernel(x_ref, shift_ref, scale_ref, w_ref, t0_ref, t1_ref, t2_ref, o_ref, h_scr,
                   *, mode, tn, n_rot_tiles, head_w, k_mult):
    j = pl.program_id(1)

    @pl.when(j == 0)
    def _():
        h = x_ref[...] * (1.0 + scale_ref[0]) + shift_ref[0]
        h_scr[...] = h.astype(BF16)

    acc = jnp.dot(h_scr[...], w_ref[...], preferred_element_type=F32)

    def rotary(mult):
        for hh in range(tn // head_w):
            lo = hh * head_w
            if mode == "ret":
                c, s = t0_ref[...], t1_ref[...]
                x1 = acc[:, lo:lo + LANES]
                x2 = acc[:, lo + LANES:lo + 2 * LANES]
                o_ref[:, lo:lo + LANES] = ((x1 * c - x2 * s) * mult).astype(o_ref.dtype)
                o_ref[:, lo + LANES:lo + 2 * LANES] = ((x2 * c + x1 * s) * mult).astype(o_ref.dtype)
            else:
                xh = acc[:, lo:lo + LANES]
                r = (xh * t0_ref[...]
                     + pltpu.roll(xh, head_w // 8, 1) * t1_ref[...]
                     + pltpu.roll(xh, LANES - head_w // 8, 1) * t2_ref[...])
                o_ref[:, lo:lo + LANES] = r.astype(o_ref.dtype)

    @pl.when(j < n_rot_tiles)
    def _():
        rotary(1.0)

    @pl.when((j >= n_rot_tiles) & (j < 2 * n_rot_tiles))
    def _():
        rotary(k_mult)

    @pl.when(j >= 2 * n_rot_tiles)
    def _():
        o_ref[...] = acc.astype(o_ref.dtype)


def _inproj(x2d, shift, scale, w_bf16, tables, *, mode, seq, qk_width, head_w, k_mult,
            tm=1024, tn=512):
    m, d = x2d.shape
    n = w_bf16.shape[1]
    assert seq % tm == 0 and n % tn == 0 and qk_width % tn == 0 and tn % head_w == 0
    tiles_per_seq = seq // tm
    t0, t1, t2 = tables
    tab_spec = pl.BlockSpec((tm, LANES), lambda i, j: (i, 0))
    mod_spec = pl.BlockSpec((1, 1, d), lambda i, j: (i // tiles_per_seq, 0, 0))
    return pl.pallas_call(
        functools.partial(_inproj_kernel, mode=mode, tn=tn, n_rot_tiles=qk_width // tn,
                          head_w=head_w, k_mult=k_mult),
        grid=(m // tm, n // tn),
        in_specs=[
            pl.BlockSpec((tm, d), lambda i, j: (i, 0)),
            mod_spec, mod_spec,
            pl.BlockSpec((d, tn), lambda i, j: (0, j)),
            tab_spec, tab_spec, tab_spec,
        ],
        out_specs=pl.BlockSpec((tm, tn), lambda i, j: (i, j)),
        out_shape=jax.ShapeDtypeStruct((m, n), BF16),
        scratch_shapes=[pltpu.VMEM((tm, d), BF16)],
        compiler_params=_params(("parallel", "arbitrary")),
        name="inproj_" + mode,
    )(x2d, shift, scale, w_bf16, t0, t1, t2)


def _retention_kernel(q_ref, k_ref, v_ref, g_ref, dec_ref, qd_ref, kd_ref, cd_ref, o_ref, st_scr,
                      *, n_chunks, chunk):
    st_scr[...] = jnp.zeros_like(st_scr)
    dec = dec_ref[...]
    qd = qd_ref[...]
    kd = kd_ref[...]
    cd = cd_ref[...]

    def body(ci, carry):
        rows = pl.ds(pl.multiple_of(ci * chunk, chunk), chunk)
        qc = q_ref[0, rows, :]
        kc = k_ref[0, rows, :]
        vc = v_ref[0, rows, :]
        state = st_scr[...]
        scores = lax.dot_general(qc, kc, (((1,), (1,)), ((), ())), preferred_element_type=F32) * dec
        inner = jnp.dot(scores.astype(BF16), vc, preferred_element_type=F32)
        cross = jnp.dot(qc, state.astype(BF16), preferred_element_type=F32) * qd
        kdec = (kc.astype(F32) * kd).astype(BF16)
        st_scr[...] = state * cd + lax.dot_general(
            kdec, vc, (((0,), (0,)), ((), ())), preferred_element_type=F32)
        o = inner + cross
        mu = jnp.mean(o, axis=-1, keepdims=True)
        oc = o - mu
        var = jnp.mean(oc * oc, axis=-1, keepdims=True)
        gn = oc * lax.rsqrt(var + RET_GN_EPS)
        g = g_ref[0, rows, :].astype(F32)
        o_ref[0, rows, :] = (gn * (g / (1.0 + jnp.exp(-g)))).astype(o_ref.dtype)
        return carry

    lax.fori_loop(0, n_chunks, body, 0)


def _retention_decays(heads, chunk):
    log_gamma = jnp.log(1.0 - 2.0 ** (-5.0 - jnp.arange(heads, dtype=F32)))
    n = jnp.arange(chunk, dtype=F32)
    diff = n[:, None] - n[None, :]
    inner = jnp.where(diff >= 0, jnp.exp(jnp.maximum(diff, 0.0) * log_gamma[:, None, None]), 0.0)
    q_decay = jnp.exp((n + 1.0) * log_gamma[:, None])
    k_decay = jnp.exp((chunk - 1.0 - n) * log_gamma[:, None])
    chunk_decay = jnp.exp(chunk * log_gamma)
    return (inner.astype(F32), q_decay.reshape(heads, chunk, 1), k_decay.reshape(heads, chunk, 1),
            chunk_decay.reshape(heads, 1, 1))


def _retention_core(proj, *, heads, dk, dv):
    b, s, _ = proj.shape
    chunk = RET_CHUNK
    assert s % chunk == 0 and dv % dk == 0
    dec, qd, kd, cd = _retention_decays(heads, chunk)
    k_off = heads
    v_off = 2 * heads * dk // dv
    g_off = v_off + heads
    return pl.pallas_call(
        functools.partial(_retention_kernel, n_chunks=s // chunk, chunk=chunk),
        grid=(b, heads),
        in_specs=[
            pl.BlockSpec((1, s, dk), lambda i, h: (i, 0, h)),
            pl.BlockSpec((1, s, dk), lambda i, h: (i, 0, k_off + h)),
            pl.BlockSpec((1, s, dv), lambda i, h: (i, 0, v_off + h)),
            pl.BlockSpec((1, s, dv), lambda i, h: (i, 0, g_off + h)),
            pl.BlockSpec((None, chunk, chunk), lambda i, h: (h, 0, 0)),
            pl.BlockSpec((None, chunk, 1), lambda i, h: (h, 0, 0)),
            pl.BlockSpec((None, chunk, 1), lambda i, h: (h, 0, 0)),
            pl.BlockSpec((None, 1, 1), lambda i, h: (h, 0, 0)),
        ],
        out_specs=pl.BlockSpec((1, s, dv), lambda i, h: (i, 0, h)),
        out_shape=jax.ShapeDtypeStruct((b, s, heads * dv), BF16),
        scratch_shapes=[pltpu.VMEM((dk, dv), F32)],
        compiler_params=_params(("parallel", "parallel")),
        name="retention_core",
    )(proj, proj, proj, proj, dec, qd, kd, cd)


def _moba_kernel(q_ref, k_ref, v_ref, g_ref, o_ref, km_scr, s_scr, *, n_blocks, blk, topk, scale):
    km_scr[...] = jnp.zeros_like(km_scr)
    for j in range(n_blocks):
        kj = k_ref[0, j * blk:(j + 1) * blk, :].astype(F32)
        km_scr[j:j + 1, :] = jnp.mean(kj, axis=0, keepdims=True)
    km = km_scr[...].astype(BF16)

    nt = (((1,), (1,)), ((), ()))
    row = lax.broadcasted_iota(jnp.int32, (blk, blk), 0)
    col = lax.broadcasted_iota(jnp.int32, (blk, blk), 1)
    for i in range(n_blocks):
        rows = slice(i * blk, (i + 1) * blk)
        qi = q_ref[0, rows, :]
        ranked = i > topk
        if ranked:
            gate = lax.dot_general(qi, km, nt, preferred_element_type=F32)
            lane = lax.broadcasted_iota(jnp.int32, gate.shape, 1)
            past = lane < i
        for j in range(i):
            s = lax.dot_general(qi, k_ref[0, j * blk:(j + 1) * blk, :], nt,
                                preferred_element_type=F32) * scale
            if ranked:
                gj = gate[:, j:j + 1]
                beats = past & ((gate > gj) | ((gate == gj) & (lane < j)))
                cnt = jnp.sum(jnp.where(beats, 1.0, 0.0), axis=-1, keepdims=True)
                s = jnp.where(cnt < topk, s, NEG)
            s_scr[:, j * blk:(j + 1) * blk] = s
        s_own = lax.dot_general(qi, k_ref[0, rows, :], nt, preferred_element_type=F32) * scale
        s_scr[:, i * blk:(i + 1) * blk] = jnp.where(col <= row, s_own, NEG)
        n_keys = (i + 1) * blk
        sl = s_scr[:, :n_keys]
        m = jnp.max(sl, axis=-1, keepdims=True)
        e = jnp.exp(sl - m)
        denom = jnp.sum(e, axis=-1, keepdims=True)
        pv = jnp.dot(e.astype(BF16), v_ref[0, :n_keys, :], preferred_element_type=F32)
        g = g_ref[0, rows, :].astype(F32)
        o_ref[0, rows, :] = ((pv / denom) * (g / (1.0 + jnp.exp(-g)))).astype(o_ref.dtype)


def _moba_core(proj, *, heads, dh):
    b, s, _ = proj.shape
    blk = MOBA_BLOCK
    assert s % blk == 0 and dh == LANES and s // blk <= LANES
    n_blocks = s // blk
    spec = lambda off: pl.BlockSpec((1, s, dh), lambda i, h: (i, 0, off + h))
    return pl.pallas_call(
        functools.partial(_moba_kernel, n_blocks=n_blocks, blk=blk,
                          topk=min(MOBA_TOPK, n_blocks), scale=dh ** -0.5),
        grid=(b, heads),
        in_specs=[spec(0), spec(heads), spec(2 * heads), spec(3 * heads)],
        out_specs=spec(0),
        out_shape=jax.ShapeDtypeStruct((b, s, heads * dh), BF16),
        scratch_shapes=[pltpu.VMEM((LANES, dh), F32), pltpu.VMEM((blk, s), F32)],
        compiler_params=_params(("parallel", "parallel")),
        name="moba_core",
    )(proj, proj, proj, proj)


def _outproj_ln_kernel(u_ref, w_ref, x_ref, gate_ref, lng_ref, lnb_ref, o_ref):
    y = jnp.dot(u_ref[...], w_ref[...], preferred_element_type=F32)
    z = DEEPNORM_ALPHA * x_ref[...] + gate_ref[0] * y
    mu = jnp.mean(z, axis=-1, keepdims=True)
    zc = z - mu
    var = jnp.mean(zc * zc, axis=-1, keepdims=True)
    o_ref[...] = zc * lax.rsqrt(var + LN_EPS) * lng_ref[...] + lnb_ref[...]


def _outproj_ln(u2d, w_bf16, x2d, gate, ln_g, ln_b, *, seq, tm=256):
    m, kdim = u2d.shape
    d = w_bf16.shape[1]
    assert seq % tm == 0
    tiles_per_seq = seq // tm
    row = pl.BlockSpec((1, d), lambda i: (0, 0))
    return pl.pallas_call(
        _outproj_ln_kernel,
        grid=(m // tm,),
        in_specs=[
            pl.BlockSpec((tm, kdim), lambda i: (i, 0)),
            pl.BlockSpec((kdim, d), lambda i: (0, 0)),
            pl.BlockSpec((tm, d), lambda i: (i, 0)),
            pl.BlockSpec((1, 1, d), lambda i: (i // tiles_per_seq, 0, 0)),
            row, row,
        ],
        out_specs=pl.BlockSpec((tm, d), lambda i: (i, 0)),
        out_shape=jax.ShapeDtypeStruct((m, d), F32),
        compiler_params=_params(("parallel",)),
        name="outproj_ln",
    )(u2d, w_bf16, x2d, gate, ln_g.reshape(1, d), ln_b.reshape(1, d))


def kernel(x, c, positions, ret_w_in, ret_w_out, moba_w_in, moba_w_out, w_ada, b_ada, ln_g, ln_b):
    b, s, d = x.shape
    ret_dk = d // RET_HEADS
    ret_dv = 2 * d // RET_HEADS
    moba_dh = d // MOBA_HEADS

    mod = _ada(c, w_ada, b_ada)
    cos_r, sin_r, mc, ms1, ms2 = [t.reshape(b * s, LANES)
                                  for t in _rope_tables(positions, ret_dk, moba_dh // 4)]

    x2d = x.reshape(b * s, d)
    for layer in range(DEPTH):
        shift = mod[layer, :, :d].reshape(b, 1, d)
        scale = mod[layer, :, d:2 * d].reshape(b, 1, d)
        gate = mod[layer, :, 2 * d:].reshape(b, 1, d)
        li = layer // N_MIXERS
        if layer % N_MIXERS == 0:
            proj = _inproj(x2d, shift, scale, ret_w_in[li].astype(BF16), (cos_r, sin_r, sin_r),
                           mode="ret", seq=s, qk_width=RET_HEADS * ret_dk, head_w=ret_dk,
                           k_mult=ret_dk ** -0.5)
            u = _retention_core(proj.reshape(b, s, -1), heads=RET_HEADS, dk=ret_dk, dv=ret_dv)
            w_out = ret_w_out[li]
        else:
            proj = _inproj(x2d, shift, scale, moba_w_in[li].astype(BF16), (mc, ms1, ms2),
                           mode="moba", seq=s, qk_width=MOBA_HEADS * moba_dh, head_w=moba_dh,
                           k_mult=1.0)
            u = _moba_core(proj.reshape(b, s, -1), heads=MOBA_HEADS, dh=moba_dh)
            w_out = moba_w_out[li]
        x2d = _outproj_ln(u.reshape(b * s, -1), w_out.astype(BF16), x2d, gate,
                          ln_g[layer], ln_b[layer], seq=s)
    return x2d.reshape(b, s, d)
```

```python
import functools
import math

import jax
import jax.numpy as jnp
import numpy as np
from jax import lax
from jax.experimental import pallas as pl
from jax.experimental.pallas import tpu as pltpu

F32 = jnp.float32
BF16 = jnp.bfloat16

DEPTH = 2
N_MIXERS = 2
DEEPNORM_ALPHA = (2.0 * DEPTH) ** 0.25
LN_EPS = 1e-5

RET_HEADS = 8
RET_CHUNK = 256
RET_THETA = 10000.0
RET_GN_EPS = 1e-5

MOBA_HEADS = 16
MOBA_BLOCK = 256
MOBA_TOPK = 3
ROPE_THETA = 500000.0
NEG = -1e30

LANES = 128
VMEM_LIMIT = 56 * 1024 * 1024


def _params(sem):
    return pltpu.CompilerParams(dimension_semantics=sem, vmem_limit_bytes=VMEM_LIMIT)


def _ada_kernel(c_ref, w_ref, b_ref, o_ref):
    acc = jnp.dot(c_ref[...].astype(BF16), w_ref[...].astype(BF16),
                  preferred_element_type=F32)
    o_ref[...] = acc + b_ref[...]


def _ada(c, w_ada, b_ada, tn=512):
    depth, d, n = w_ada.shape
    b = c.shape[0]
    return pl.pallas_call(
        _ada_kernel,
        grid=(depth, n // tn),
        in_specs=[
            pl.BlockSpec((b, d), lambda l, j: (0, 0)),
            pl.BlockSpec((None, d, tn), lambda l, j: (l, 0, j)),
            pl.BlockSpec((None, 1, tn), lambda l, j: (l, 0, j)),
        ],
        out_specs=pl.BlockSpec((None, b, tn), lambda l, j: (l, 0, j)),
        out_shape=jax.ShapeDtypeStruct((depth, b, n), F32),
        compiler_params=_params(("parallel", "parallel")),
        name="ada_mod",
    )(c, w_ada, b_ada.reshape(depth, 1, n))


def _rope_tables_kernel(pos_ref, inv_r_ref, inv_m_ref, cos_r, sin_r, mc, ms, *, half_m):
    pos = pos_ref[0].astype(F32)
    ang = pos * inv_r_ref[...]
    cos_r[0] = jnp.cos(ang)
    sin_r[0] = jnp.sin(ang)
    angm = pos * inv_m_ref[...]
    cm = jnp.cos(angm)
    sm = jnp.sin(angm)
    lane = lax.broadcasted_iota(jnp.int32, angm.shape, 1)
    first = lane < half_m
    second = (lane >= LANES // 2) & (lane < LANES // 2 + half_m)
    mc[0] = jnp.where(first | second, cm, 1.0)
    ms[0] = jnp.where(first, -sm, jnp.where(second, sm, 0.0))


def _rope_tables(positions, ret_dk, moba_rot, ts=512):
    b, s = positions.shape
    half_r = ret_dk // 2
    half_m = moba_rot // 2
    assert half_r == LANES and LANES % half_m == 0
    inv_r = RET_THETA ** (-jnp.arange(half_r, dtype=F32) * 2.0 / ret_dk)
    inv_m = ROPE_THETA ** (-jnp.arange(half_m, dtype=F32) * 2.0 / moba_rot)
    inv_m = jnp.tile(inv_m, LANES // half_m)
    tab = jax.ShapeDtypeStruct((b, s, LANES), F32)
    spec = pl.BlockSpec((1, ts, LANES), lambda i, j: (i, j, 0))
    row = pl.BlockSpec((1, LANES), lambda i, j: (0, 0))
    return pl.pallas_call(
        functools.partial(_rope_tables_kernel, half_m=half_m),
        grid=(b, s // ts),
        in_specs=[pl.BlockSpec((1, ts, 1), lambda i, j: (i, j, 0)), row, row],
        out_specs=[spec] * 4,
        out_shape=[tab] * 4,
        compiler_params=_params(("parallel", "parallel")),
        name="rope_tables",
    )(positions.reshape(b, s, 1), inv_r.reshape(1, LANES), inv_m.reshape(1, LANES))


def _moba_qk_column_order(n_heads, dh, half):
    idx = np.arange(dh)
    idx[half:2 * half] = np.arange(LANES // 2, LANES // 2 + half)
    idx[LANES // 2:LANES // 2 + half] = np.arange(half, 2 * half)
    return (np.arange(n_heads)[:, None] * dh + idx[None, :]).reshape(-1)


def _inproj_ret_kernel(x_ref, shift_ref, scale_ref, w_ref, cos_ref, sin_ref, o_ref, h_scr,
                       *, tn, n_rot_tiles, head_w, k_mult):
    j = pl.program_id(1)

    @pl.when(j == 0)
    def _():
        h = x_ref[...] * (1.0 + scale_ref[0]) + shift_ref[0]
        h_scr[...] = h.astype(BF16)

    acc = jnp.dot(h_scr[...], w_ref[...], preferred_element_type=F32)

    def rotary(mult):
        c, s = cos_ref[...], sin_ref[...]
        for hh in range(tn // head_w):
            lo = hh * head_w
            x1 = acc[:, lo:lo + LANES]
            x2 = acc[:, lo + LANES:lo + 2 * LANES]
            o_ref[:, lo:lo + LANES] = ((x1 * c - x2 * s) * mult).astype(o_ref.dtype)
            o_ref[:, lo + LANES:lo + 2 * LANES] = ((x2 * c + x1 * s) * mult).astype(o_ref.dtype)

    @pl.when(j < n_rot_tiles)
    def _():
        rotary(1.0)

    @pl.when((j >= n_rot_tiles) & (j < 2 * n_rot_tiles))
    def _():
        rotary(k_mult)

    @pl.when(j >= 2 * n_rot_tiles)
    def _():
        o_ref[...] = acc.astype(o_ref.dtype)


def _inproj_ret(x2d, shift, scale, w_bf16, cos_t, sin_t, *, seq, qk_width, head_w, k_mult,
                tm=1024, tn=1024):
    m, d = x2d.shape
    n = w_bf16.shape[1]
    assert seq % tm == 0 and n % tn == 0 and qk_width % tn == 0 and tn % head_w == 0
    assert head_w == 2 * LANES
    tiles_per_seq = seq // tm
    tab_spec = pl.BlockSpec((tm, LANES), lambda i, j: (i, 0))
    mod_spec = pl.BlockSpec((1, 1, d), lambda i, j: (i // tiles_per_seq, 0, 0))
    return pl.pallas_call(
        functools.partial(_inproj_ret_kernel, tn=tn, n_rot_tiles=qk_width // tn,
                          head_w=head_w, k_mult=k_mult),
        grid=(m // tm, n // tn),
        in_specs=[
            pl.BlockSpec((tm, d), lambda i, j: (i, 0)),
            mod_spec, mod_spec,
            pl.BlockSpec((d, tn), lambda i, j: (0, j)),
            tab_spec, tab_spec,
        ],
        out_specs=pl.BlockSpec((tm, tn), lambda i, j: (i, j)),
        out_shape=jax.ShapeDtypeStruct((m, n), BF16),
        scratch_shapes=[pltpu.VMEM((tm, d), BF16)],
        compiler_params=_params(("parallel", "arbitrary")),
        name="inproj_ret",
    )(x2d, shift, scale, w_bf16, cos_t, sin_t)


def _matmul_kernel(h_ref, w_ref, o_ref):
    o_ref[...] = jnp.dot(h_ref[...], w_ref[...], preferred_element_type=F32).astype(o_ref.dtype)


def _inproj_plain(h2d, w_bf16, *, tm=2048, tn=1024):
    m, d = h2d.shape
    n = w_bf16.shape[1]
    assert m % tm == 0 and n % tn == 0
    return pl.pallas_call(
        _matmul_kernel,
        grid=(m // tm, n // tn),
        in_specs=[pl.BlockSpec((tm, d), lambda i, j: (i, 0)),
                  pl.BlockSpec((d, tn), lambda i, j: (0, j))],
        out_specs=pl.BlockSpec((tm, tn), lambda i, j: (i, j)),
        out_shape=jax.ShapeDtypeStruct((m, n), BF16),
        compiler_params=_params(("parallel", "parallel")),
        name="inproj_plain",
    )(h2d, w_bf16)


def _retention_kernel(q_ref, k_ref, v_ref, g_ref, dec_ref, qd_ref, kd_ref, cd_ref, o_ref, st_scr,
                      *, n_chunks, chunk):
    st_scr[...] = jnp.zeros_like(st_scr)
    dec = dec_ref[...]
    qd = qd_ref[...]
    kd = kd_ref[...]
    cd = cd_ref[...]

    def body(ci, carry):
        rows = pl.ds(pl.multiple_of(ci * chunk, chunk), chunk)
        qc = q_ref[0, rows, :]
        kc = k_ref[0, rows, :]
        vc = v_ref[0, rows, :]
        state = st_scr[...]
        scores = lax.dot_general(qc, kc, (((1,), (1,)), ((), ())), preferred_element_type=F32) * dec
        inner = jnp.dot(scores.astype(BF16), vc, preferred_element_type=F32)
        cross = jnp.dot(qc, state.astype(BF16), preferred_element_type=F32) * qd
        kdec = (kc.astype(F32) * kd).astype(BF16)
        st_scr[...] = state * cd + lax.dot_general(
            kdec, vc, (((0,), (0,)), ((), ())), preferred_element_type=F32)
        o = inner + cross
        mu = jnp.mean(o, axis=-1, keepdims=True)
        oc = o - mu
        var = jnp.mean(oc * oc, axis=-1, keepdims=True)
        gn = oc * lax.rsqrt(var + RET_GN_EPS)
        g = g_ref[0, rows, :].astype(F32)
        o_ref[0, rows, :] = (gn * (g / (1.0 + jnp.exp(-g)))).astype(o_ref.dtype)
        return carry

    lax.fori_loop(0, n_chunks, body, 0)


def _retention_decays(heads, chunk):
    log_gamma = jnp.log(1.0 - 2.0 ** (-5.0 - jnp.arange(heads, dtype=F32)))
    n = jnp.arange(chunk, dtype=F32)
    diff = n[:, None] - n[None, :]
    inner = jnp.where(diff >= 0, jnp.exp(jnp.maximum(diff, 0.0) * log_gamma[:, None, None]), 0.0)
    q_decay = jnp.exp((n + 1.0) * log_gamma[:, None])
    k_decay = jnp.exp((chunk - 1.0 - n) * log_gamma[:, None])
    chunk_decay = jnp.exp(chunk * log_gamma)
    return (inner.astype(F32), q_decay.reshape(heads, chunk, 1), k_decay.reshape(heads, chunk, 1),
            chunk_decay.reshape(heads, 1, 1))


def _retention_core(proj, *, heads, dk, dv):
    b, s, _ = proj.shape
    chunk = RET_CHUNK
    assert s % chunk == 0 and dv % dk == 0
    dec, qd, kd, cd = _retention_decays(heads, chunk)
    k_off = heads
    v_off = 2 * heads * dk // dv
    g_off = v_off + heads
    return pl.pallas_call(
        functools.partial(_retention_kernel, n_chunks=s // chunk, chunk=chunk),
        grid=(b, heads),
        in_specs=[
            pl.BlockSpec((1, s, dk), lambda i, h: (i, 0, h)),
            pl.BlockSpec((1, s, dk), lambda i, h: (i, 0, k_off + h)),
            pl.BlockSpec((1, s, dv), lambda i, h: (i, 0, v_off + h)),
            pl.BlockSpec((1, s, dv), lambda i, h: (i, 0, g_off + h)),
            pl.BlockSpec((None, chunk, chunk), lambda i, h: (h, 0, 0)),
            pl.BlockSpec((None, chunk, 1), lambda i, h: (h, 0, 0)),
            pl.BlockSpec((None, chunk, 1), lambda i, h: (h, 0, 0)),
            pl.BlockSpec((None, 1, 1), lambda i, h: (h, 0, 0)),
        ],
        out_specs=pl.BlockSpec((1, s, dv), lambda i, h: (i, 0, h)),
        out_shape=jax.ShapeDtypeStruct((b, s, heads * dv), BF16),
        scratch_shapes=[pltpu.VMEM((dk, dv), F32)],
        compiler_params=_params(("parallel", "parallel")),
        name="retention_core",
    )(proj, proj, proj, proj, dec, qd, kd, cd)


def _moba_kernel(q_ref, k_ref, v_ref, g_ref, mc_ref, ms_ref, o_ref,
                 qr_scr, kr_scr, vt_scr, km_scr, *s_scrs, n_blocks, blk, topk, exp_scale):
    mc = mc_ref[0]
    ms = ms_ref[0]

    def rotary(x_ref):
        x = x_ref[0].astype(F32)
        return x * mc + pltpu.roll(x, LANES // 2, 1) * ms

    qr_scr[...] = (rotary(q_ref) * exp_scale).astype(BF16)
    kr_scr[...] = rotary(k_ref).astype(BF16)
    vt_scr[...] = v_ref[0].astype(F32).T.astype(BF16)
    for j in range(n_blocks):
        kj = kr_scr[j * blk:(j + 1) * blk, :].astype(F32)
        km_scr[j:j + 1, :] = jnp.mean(kj, axis=0, keepdims=True)

    nt = (((1,), (1,)), ((), ()))
    gate_t = lax.dot_general(km_scr[...].astype(BF16), qr_scr[...], nt,
                             preferred_element_type=F32)
    key_id = lax.broadcasted_iota(jnp.int32, (blk, blk), 0)
    qry_id = lax.broadcasted_iota(jnp.int32, (blk, blk), 1)
    blk_id = lax.broadcasted_iota(jnp.int32, (n_blocks, blk), 0)

    def masked_scores(i):
        cols = slice(i * blk, (i + 1) * blk)
        n_keys = (i + 1) * blk
        s_scr = s_scrs[i]
        st = lax.dot_general(kr_scr[:n_keys, :], qr_scr[cols, :], nt,
                             preferred_element_type=F32)
        ranked = i > topk
        if ranked:
            gi = gate_t[:, cols]
            past = blk_id < i
        m = None
        for j in range(i + 1):
            sj = st[j * blk:(j + 1) * blk, :]
            if j == i:
                sj = jnp.where(key_id <= qry_id, sj, NEG)
            elif ranked:
                gj = gi[j:j + 1, :]
                beats = past & ((gi > gj) | ((gi == gj) & (blk_id < j)))
                cnt = jnp.sum(jnp.where(beats, 1.0, 0.0), axis=0, keepdims=True)
                sj = sj + jnp.where(cnt < topk, 0.0, NEG)
            s_scr[j * blk:(j + 1) * blk, :] = sj
            mj = jnp.max(sj, axis=0, keepdims=True)
            m = mj if m is None else jnp.maximum(m, mj)
        return m

    def attend(i, m):
        cols = slice(i * blk, (i + 1) * blk)
        n_keys = (i + 1) * blk
        e = jnp.exp2(s_scrs[i][...] - m)
        denom = jnp.sum(e, axis=0, keepdims=True)
        ot = jnp.dot(vt_scr[:, :n_keys], e.astype(BF16), preferred_element_type=F32)
        o = (ot * (1.0 / denom)).T
        g = g_ref[0, cols, :].astype(F32)
        o_ref[0, cols, :] = (o * (g / (1.0 + jnp.exp(-g)))).astype(o_ref.dtype)

    m_next = masked_scores(0)
    for i in range(n_blocks):
        m_cur = m_next
        if i + 1 < n_blocks:
            m_next = masked_scores(i + 1)
        attend(i, m_cur)


def _moba_core(proj, mc, ms, *, heads, dh):
    b, s, _ = proj.shape
    blk = MOBA_BLOCK
    assert s % blk == 0 and dh == LANES
    n_blocks = s // blk
    spec = lambda off: pl.BlockSpec((1, s, dh), lambda i, h: (i, 0, off + h))
    tab = pl.BlockSpec((1, s, dh), lambda i, h: (i, 0, 0))
    return pl.pallas_call(
        functools.partial(_moba_kernel, n_blocks=n_blocks, blk=blk,
                          topk=min(MOBA_TOPK, n_blocks),
                          exp_scale=dh ** -0.5 * math.log2(math.e)),
        grid=(b, heads),
        in_specs=[spec(0), spec(heads), spec(2 * heads), spec(3 * heads), tab, tab],
        out_specs=spec(0),
        out_shape=jax.ShapeDtypeStruct((b, s, heads * dh), BF16),
        scratch_shapes=[pltpu.VMEM((s, dh), BF16), pltpu.VMEM((s, dh), BF16),
                        pltpu.VMEM((dh, s), BF16), pltpu.VMEM((n_blocks, dh), F32)]
        + [pltpu.VMEM(((i + 1) * blk, blk), F32) for i in range(n_blocks)],
        compiler_params=_params(("parallel", "parallel")),
        name="moba_core",
    )(proj, proj, proj, proj, mc, ms)


def _outproj_ln_kernel(u_ref, w_ref, x_ref, gate_ref, lng_ref, lnb_ref, *rest, sub, emit_h):
    if emit_h:
        nshift_ref, nscale_ref, o_ref, h_ref = rest
    else:
        (o_ref,) = rest
    n_sub = u_ref.shape[0] // sub

    def matmul(r):
        return jnp.dot(u_ref[r * sub:(r + 1) * sub, :], w_ref[...], preferred_element_type=F32)

    y_next = matmul(0)
    for r in range(n_sub):
        rows = slice(r * sub, (r + 1) * sub)
        y = y_next
        if r + 1 < n_sub:
            y_next = matmul(r + 1)
        o_ref[rows, :] = DEEPNORM_ALPHA * x_ref[rows, :] + gate_ref[0] * y
        mu = jnp.mean(o_ref[rows, :], axis=-1, keepdims=True)
        var = jnp.mean(jnp.square(o_ref[rows, :] - mu), axis=-1, keepdims=True)
        xn = (o_ref[rows, :] - mu) * lax.rsqrt(var + LN_EPS) * lng_ref[...] + lnb_ref[...]
        o_ref[rows, :] = xn
        if emit_h:
            h_ref[rows, :] = (xn * (1.0 + nscale_ref[0]) + nshift_ref[0]).astype(h_ref.dtype)


def _outproj_ln(u2d, w_bf16, x2d, gate, ln_g, ln_b, next_mod=None, *, seq, tm, sub=256):
    m, kdim = u2d.shape
    d = w_bf16.shape[1]
    assert seq % tm == 0 and tm % sub == 0
    tiles_per_seq = seq // tm
    row = pl.BlockSpec((1, d), lambda i: (0, 0))
    mod_spec = pl.BlockSpec((1, 1, d), lambda i: (i // tiles_per_seq, 0, 0))
    tile = pl.BlockSpec((tm, d), lambda i: (i, 0))
    emit_h = next_mod is not None
    in_specs = [
        pl.BlockSpec((tm, kdim), lambda i: (i, 0)),
        pl.BlockSpec((kdim, d), lambda i: (0, 0), pipeline_mode=pl.Buffered(1)),
        tile, mod_spec, row, row,
    ]
    args = [u2d, w_bf16, x2d, gate, ln_g.reshape(1, d), ln_b.reshape(1, d)]
    out_specs, out_shape = tile, jax.ShapeDtypeStruct((m, d), F32)
    if emit_h:
        in_specs += [mod_spec, mod_spec]
        args += list(next_mod)
        out_specs = [tile, tile]
        out_shape = [out_shape, jax.ShapeDtypeStruct((m, d), BF16)]
    return pl.pallas_call(
        functools.partial(_outproj_ln_kernel, sub=sub, emit_h=emit_h),
        grid=(m // tm,),
        in_specs=in_specs,
        out_specs=out_specs,
        out_shape=out_shape,
        compiler_params=_params(("parallel",)),
        name="outproj_ln",
    )(*args)


def kernel(x, c, positions, ret_w_in, ret_w_out, moba_w_in, moba_w_out, w_ada, b_ada, ln_g, ln_b):
    b, s, d = x.shape
    assert DEPTH == 2 and w_ada.shape[0] == DEPTH
    ret_dk = d // RET_HEADS
    ret_dv = 2 * d // RET_HEADS
    moba_dh = d // MOBA_HEADS
    moba_half = moba_dh // 8

    mod = _ada(c, w_ada, b_ada)
    shift, scale, gate = [[mod[l, :, k * d:(k + 1) * d].reshape(b, 1, d) for l in range(DEPTH)]
                          for k in range(3)]
    cos_r, sin_r, mc, ms = _rope_tables(positions, ret_dk, 2 * moba_half)

    x2d = x.reshape(b * s, d)
    proj = _inproj_ret(x2d, shift[0], scale[0], ret_w_in[0].astype(BF16),
                       cos_r.reshape(b * s, LANES), sin_r.reshape(b * s, LANES),
                       seq=s, qk_width=RET_HEADS * ret_dk, head_w=ret_dk, k_mult=ret_dk ** -0.5)
    u = _retention_core(proj.reshape(b, s, -1), heads=RET_HEADS, dk=ret_dk, dv=ret_dv)
    x2d, h2d = _outproj_ln(u.reshape(b * s, -1), ret_w_out[0].astype(BF16), x2d, gate[0],
                           ln_g[0], ln_b[0], next_mod=(shift[1], scale[1]), seq=s, tm=256)

    qk_cols = _moba_qk_column_order(2 * MOBA_HEADS, moba_dh, moba_half)
    w_in = moba_w_in[0]
    w_in = jnp.concatenate([w_in[:, qk_cols], w_in[:, qk_cols.size:]], axis=1).astype(BF16)
    proj = _inproj_plain(h2d, w_in)
    u = _moba_core(proj.reshape(b, s, -1), mc, ms, heads=MOBA_HEADS, dh=moba_dh)
    x2d = _outproj_ln(u.reshape(b * s, -1), moba_w_out[0].astype(BF16), x2d, gate[1],
                      ln_g[1], ln_b[1], seq=s, tm=512)
    return x2d.reshape(b, s, d)
```

```python
import functools
import math

import jax
import jax.numpy as jnp
import numpy as np
from jax import lax
from jax.experimental import pallas as pl
from jax.experimental.pallas import tpu as pltpu

F32 = jnp.float32
BF16 = jnp.bfloat16

DEPTH = 2
N_MIXERS = 2
DEEPNORM_ALPHA = (2.0 * DEPTH) ** 0.25
LN_EPS = 1e-5

RET_HEADS = 8
RET_CHUNK = 256
RET_THETA = 10000.0
RET_GN_EPS = 1e-5

MOBA_HEADS = 16
MOBA_BLOCK = 256
MOBA_TOPK = 3
ROPE_THETA = 500000.0
NEG = -1e30

LANES = 128
VMEM_LIMIT = 56 * 1024 * 1024


def _params(sem):
    return pltpu.CompilerParams(dimension_semantics=sem, vmem_limit_bytes=VMEM_LIMIT)


def _ada_kernel(c_ref, w_ref, b_ref, o_ref):
    acc = jnp.dot(c_ref[...].astype(BF16), w_ref[...].astype(BF16),
                  preferred_element_type=F32)
    o_ref[...] = acc + b_ref[...]


def _ada(c, w_ada, b_ada, tn=512):
    depth, d, n = w_ada.shape
    b = c.shape[0]
    return pl.pallas_call(
        _ada_kernel,
        grid=(depth, n // tn),
        in_specs=[
            pl.BlockSpec((b, d), lambda l, j: (0, 0)),
            pl.BlockSpec((None, d, tn), lambda l, j: (l, 0, j)),
            pl.BlockSpec((None, 1, tn), lambda l, j: (l, 0, j)),
        ],
        out_specs=pl.BlockSpec((None, b, tn), lambda l, j: (l, 0, j)),
        out_shape=jax.ShapeDtypeStruct((depth, b, n), F32),
        compiler_params=_params(("parallel", "parallel")),
        name="ada_mod",
    )(c, w_ada, b_ada.reshape(depth, 1, n))


def _rope_tables_kernel(pos_ref, inv_r_ref, inv_m_ref, cos_r, sin_r, mc, ms, *, half_m):
    pos = pos_ref[0].astype(F32)
    ang = pos * inv_r_ref[...]
    cos_r[0] = jnp.cos(ang)
    sin_r[0] = jnp.sin(ang)
    angm = pos * inv_m_ref[...]
    cm = jnp.cos(angm)
    sm = jnp.sin(angm)
    lane = lax.broadcasted_iota(jnp.int32, angm.shape, 1)
    first = lane < half_m
    second = (lane >= LANES // 2) & (lane < LANES // 2 + half_m)
    mc[0] = jnp.where(first | second, cm, 1.0)
    ms[0] = jnp.where(first, -sm, jnp.where(second, sm, 0.0))


def _rope_tables(positions, ret_dk, moba_rot, ts=512):
    b, s = positions.shape
    half_r = ret_dk // 2
    half_m = moba_rot // 2
    assert half_r == LANES and LANES % half_m == 0
    inv_r = RET_THETA ** (-jnp.arange(half_r, dtype=F32) * 2.0 / ret_dk)
    inv_m = ROPE_THETA ** (-jnp.arange(half_m, dtype=F32) * 2.0 / moba_rot)
    inv_m = jnp.tile(inv_m, LANES // half_m)
    tab = jax.ShapeDtypeStruct((b, s, LANES), F32)
    spec = pl.BlockSpec((1, ts, LANES), lambda i, j: (i, j, 0))
    row = pl.BlockSpec((1, LANES), lambda i, j: (0, 0))
    return pl.pallas_call(
        functools.partial(_rope_tables_kernel, half_m=half_m),
        grid=(b, s // ts),
        in_specs=[pl.BlockSpec((1, ts, 1), lambda i, j: (i, j, 0)), row, row],
        out_specs=[spec] * 4,
        out_shape=[tab] * 4,
        compiler_params=_params(("parallel", "parallel")),
        name="rope_tables",
    )(positions.reshape(b, s, 1), inv_r.reshape(1, LANES), inv_m.reshape(1, LANES))


def _moba_w_in_bf16(w, n_qk_heads, dh, half):
    d = w.shape[0]
    mid = LANES // 2
    wqk = w[:, :n_qk_heads * dh].reshape(d, n_qk_heads, dh)
    wqk = jnp.concatenate([wqk[..., :half], wqk[..., mid:mid + half], wqk[..., 2 * half:mid],
                           wqk[..., half:2 * half], wqk[..., mid + half:]], axis=-1)
    return jnp.concatenate([wqk.reshape(d, n_qk_heads * dh), w[:, n_qk_heads * dh:]],
                           axis=1).astype(BF16)


def _inproj_qk_kernel(x_ref, shift_ref, scale_ref, w_ref, cos_ref, sin_ref, o_ref, h_ref,
                      *, tn, n_q_tiles, head_w, k_mult):
    j = pl.program_id(1)

    @pl.when(j == 0)
    def _():
        h = x_ref[...] * (1.0 + scale_ref[0]) + shift_ref[0]
        h_ref[...] = h.astype(h_ref.dtype)

    acc = jnp.dot(h_ref[...], w_ref[...], preferred_element_type=F32)
    mult = jnp.where(j < n_q_tiles, 1.0, k_mult)
    c = cos_ref[...] * mult
    s = sin_ref[...] * mult
    for hh in range(tn // head_w):
        lo = hh * head_w
        x1 = acc[:, lo:lo + LANES]
        x2 = acc[:, lo + LANES:lo + 2 * LANES]
        o_ref[:, lo:lo + LANES] = (x1 * c - x2 * s).astype(o_ref.dtype)
        o_ref[:, lo + LANES:lo + 2 * LANES] = (x2 * c + x1 * s).astype(o_ref.dtype)


def _inproj_qk(x2d, shift, scale, w_bf16, cos_t, sin_t, *, seq, head_w, k_mult, tm=1024, tn=1024):
    m, d = x2d.shape
    n = w_bf16.shape[1]
    assert seq % tm == 0 and (n // 2) % tn == 0 and tn % head_w == 0 and head_w == 2 * LANES
    tiles_per_seq = seq // tm
    tab_spec = pl.BlockSpec((tm, LANES), lambda i, j: (i, 0))
    mod_spec = pl.BlockSpec((1, 1, d), lambda i, j: (i // tiles_per_seq, 0, 0))
    return pl.pallas_call(
        functools.partial(_inproj_qk_kernel, tn=tn, n_q_tiles=n // 2 // tn,
                          head_w=head_w, k_mult=k_mult),
        grid=(m // tm, n // tn),
        in_specs=[
            pl.BlockSpec((tm, d), lambda i, j: (i, 0)),
            mod_spec, mod_spec,
            pl.BlockSpec((d, tn), lambda i, j: (0, j)),
            tab_spec, tab_spec,
        ],
        out_specs=[pl.BlockSpec((tm, tn), lambda i, j: (i, j)),
                   pl.BlockSpec((tm, d), lambda i, j: (i, 0))],
        out_shape=[jax.ShapeDtypeStruct((m, n), BF16), jax.ShapeDtypeStruct((m, d), BF16)],
        compiler_params=_params(("parallel", "arbitrary")),
        name="inproj_qk",
    )(x2d, shift, scale, w_bf16, cos_t, sin_t)


def _matmul_kernel(h_ref, w_ref, o_ref):
    o_ref[...] = jnp.dot(h_ref[...], w_ref[...], preferred_element_type=F32).astype(o_ref.dtype)


def _inproj_plain(h2d, w_bf16, *, tm=2048, tn=1024):
    m, d = h2d.shape
    n = w_bf16.shape[1]
    assert m % tm == 0 and n % tn == 0
    return pl.pallas_call(
        _matmul_kernel,
        grid=(m // tm, n // tn),
        in_specs=[pl.BlockSpec((tm, d), lambda i, j: (i, 0)),
                  pl.BlockSpec((d, tn), lambda i, j: (0, j))],
        out_specs=pl.BlockSpec((tm, tn), lambda i, j: (i, j)),
        out_shape=jax.ShapeDtypeStruct((m, n), BF16),
        compiler_params=_params(("parallel", "parallel")),
        name="inproj_plain",
    )(h2d, w_bf16)


def _retention_kernel(q_ref, k_ref, v_ref, g_ref, dec_ref, qd_ref, kd_ref, cd_ref, o_ref, st_scr,
                      *, n_chunks, chunk):
    st_scr[...] = jnp.zeros_like(st_scr)
    qd = qd_ref[...]
    cd = cd_ref[...]

    def state_free(ci):
        rows = slice(ci * chunk, (ci + 1) * chunk)
        qc = q_ref[0, rows, :]
        kc = k_ref[0, rows, :]
        vc = v_ref[0, rows, :]
        scores = lax.dot_general(qc, kc, (((1,), (1,)), ((), ())),
                                 preferred_element_type=F32) * dec_ref[...]
        inner = jnp.dot(scores.astype(BF16), vc, preferred_element_type=F32)
        kdec = (kc.astype(F32) * kd_ref[...]).astype(BF16)
        update = lax.dot_general(kdec, vc, (((0,), (0,)), ((), ())), preferred_element_type=F32)
        return inner, update

    def finish(ci, inner, update):
        rows = slice(ci * chunk, (ci + 1) * chunk)
        state = st_scr[...]
        cross = jnp.dot(q_ref[0, rows, :], state.astype(BF16), preferred_element_type=F32) * qd
        st_scr[...] = state * cd + update
        o = inner + cross
        mu = jnp.mean(o, axis=-1, keepdims=True)
        oc = o - mu
        var = jnp.mean(oc * oc, axis=-1, keepdims=True)
        gn = oc * lax.rsqrt(var + RET_GN_EPS)
        g = g_ref[0, rows, :].astype(F32)
        o_ref[0, rows, :] = (gn * (g / (1.0 + jnp.exp(-g)))).astype(o_ref.dtype)

    nxt = state_free(0)
    for ci in range(n_chunks):
        cur = nxt
        if ci + 1 < n_chunks:
            nxt = state_free(ci + 1)
        finish(ci, *cur)


def _retention_decays(heads, chunk):
    log_gamma = jnp.log(1.0 - 2.0 ** (-5.0 - jnp.arange(heads, dtype=F32)))
    n = jnp.arange(chunk, dtype=F32)
    diff = n[:, None] - n[None, :]
    inner = jnp.where(diff >= 0, jnp.exp(jnp.maximum(diff, 0.0) * log_gamma[:, None, None]), 0.0)
    q_decay = jnp.exp((n + 1.0) * log_gamma[:, None])
    k_decay = jnp.exp((chunk - 1.0 - n) * log_gamma[:, None])
    chunk_decay = jnp.exp(chunk * log_gamma)
    return (inner.astype(F32), q_decay.reshape(heads, chunk, 1), k_decay.reshape(heads, chunk, 1),
            chunk_decay.reshape(heads, 1, 1))


def _retention_core(qk, vg, *, heads, dk, dv):
    b, s, _ = qk.shape
    chunk = RET_CHUNK
    assert s % chunk == 0
    dec, qd, kd, cd = _retention_decays(heads, chunk)
    return pl.pallas_call(
        functools.partial(_retention_kernel, n_chunks=s // chunk, chunk=chunk),
        grid=(b, heads),
        in_specs=[
            pl.BlockSpec((1, s, dk), lambda i, h: (i, 0, h)),
            pl.BlockSpec((1, s, dk), lambda i, h: (i, 0, heads + h)),
            pl.BlockSpec((1, s, dv), lambda i, h: (i, 0, h)),
            pl.BlockSpec((1, s, dv), lambda i, h: (i, 0, heads + h)),
            pl.BlockSpec((None, chunk, chunk), lambda i, h: (h, 0, 0)),
            pl.BlockSpec((None, chunk, 1), lambda i, h: (h, 0, 0)),
            pl.BlockSpec((None, chunk, 1), lambda i, h: (h, 0, 0)),
            pl.BlockSpec((None, 1, 1), lambda i, h: (h, 0, 0)),
        ],
        out_specs=pl.BlockSpec((1, s, dv), lambda i, h: (i, 0, h)),
        out_shape=jax.ShapeDtypeStruct((b, s, heads * dv), BF16),
        scratch_shapes=[pltpu.VMEM((dk, dv), F32)],
        compiler_params=_params(("parallel", "parallel")),
        name="retention_core",
    )(qk, qk, vg, vg, dec, qd, kd, cd)


def _moba_kernel(q_ref, k_ref, v_ref, g_ref, mc_ref, ms_ref, o_ref,
                 qr_scr, kr_scr, vt_scr, km_scr, *s_scrs, n_blocks, blk, topk, exp_scale):
    mc = mc_ref[0]
    ms = ms_ref[0]

    def rotary(x_ref):
        x = x_ref[0].astype(F32)
        return x * mc + pltpu.roll(x, LANES // 2, 1) * ms

    qr_scr[...] = (rotary(q_ref) * exp_scale).astype(BF16)
    kr_scr[...] = rotary(k_ref).astype(BF16)
    vt_scr[...] = v_ref[0].astype(F32).T.astype(BF16)
    for j in range(n_blocks):
        kj = kr_scr[j * blk:(j + 1) * blk, :].astype(F32)
        km_scr[j:j + 1, :] = jnp.mean(kj, axis=0, keepdims=True)

    nt = (((1,), (1,)), ((), ()))
    gate_t = lax.dot_general(km_scr[...].astype(BF16), qr_scr[...], nt,
                             preferred_element_type=F32)
    key_id = lax.broadcasted_iota(jnp.int32, (blk, blk), 0)
    qry_id = lax.broadcasted_iota(jnp.int32, (blk, blk), 1)
    blk_id = lax.broadcasted_iota(jnp.int32, (n_blocks, blk), 0)

    def masked_scores(i):
        cols = slice(i * blk, (i + 1) * blk)
        n_keys = (i + 1) * blk
        s_scr = s_scrs[i]
        st = lax.dot_general(kr_scr[:n_keys, :], qr_scr[cols, :], nt,
                             preferred_element_type=F32)
        ranked = i > topk
        if ranked:
            gi = gate_t[:, cols]
            past = blk_id < i
        m = None
        for j in range(i + 1):
            sj = st[j * blk:(j + 1) * blk, :]
            if j == i:
                sj = jnp.where(key_id <= qry_id, sj, NEG)
            elif ranked:
                gj = gi[j:j + 1, :]
                beats = past & ((gi > gj) | ((gi == gj) & (blk_id < j)))
                cnt = jnp.sum(jnp.where(beats, 1.0, 0.0), axis=0, keepdims=True)
                sj = sj + jnp.where(cnt < topk, 0.0, NEG)
            s_scr[j * blk:(j + 1) * blk, :] = sj
            mj = jnp.max(sj, axis=0, keepdims=True)
            m = mj if m is None else jnp.maximum(m, mj)
        return m

    def attend(i, m):
        cols = slice(i * blk, (i + 1) * blk)
        n_keys = (i + 1) * blk
        e = jnp.exp2(s_scrs[i][...] - m)
        denom = jnp.sum(e, axis=0, keepdims=True)
        ot = jnp.dot(vt_scr[:, :n_keys], e.astype(BF16), preferred_element_type=F32)
        o = (ot * (1.0 / denom)).T
        g = g_ref[0, cols, :].astype(F32)
        o_ref[0, cols, :] = (o * (g / (1.0 + jnp.exp(-g)))).astype(o_ref.dtype)

    m_next = masked_scores(0)
    for i in range(n_blocks):
        m_cur = m_next
        if i + 1 < n_blocks:
            m_next = masked_scores(i + 1)
        attend(i, m_cur)


def _moba_core(proj, mc, ms, *, heads, dh):
    b, s, _ = proj.shape
    blk = MOBA_BLOCK
    assert s % blk == 0 and dh == LANES
    n_blocks = s // blk
    spec = lambda off: pl.BlockSpec((1, s, dh), lambda i, h: (i, 0, off + h))
    tab = pl.BlockSpec((1, s, dh), lambda i, h: (i, 0, 0))
    return pl.pallas_call(
        functools.partial(_moba_kernel, n_blocks=n_blocks, blk=blk,
                          topk=min(MOBA_TOPK, n_blocks),
                          exp_scale=dh ** -0.5 * math.log2(math.e)),
        grid=(b, heads),
        in_specs=[spec(0), spec(heads), spec(2 * heads), spec(3 * heads), tab, tab],
        out_specs=spec(0),
        out_shape=jax.ShapeDtypeStruct((b, s, heads * dh), BF16),
        scratch_shapes=[pltpu.VMEM((s, dh), BF16), pltpu.VMEM((s, dh), BF16),
                        pltpu.VMEM((dh, s), BF16), pltpu.VMEM((n_blocks, dh), F32)]
        + [pltpu.VMEM(((i + 1) * blk, blk), F32) for i in range(n_blocks)],
        compiler_params=_params(("parallel", "parallel")),
        name="moba_core",
    )(proj, proj, proj, proj, mc, ms)


def _outproj_ln_kernel(u_ref, w_ref, x_ref, gate_ref, lng_ref, lnb_ref, *rest, sub, emit_h):
    if emit_h:
        nshift_ref, nscale_ref, o_ref, h_ref = rest
    else:
        (o_ref,) = rest
    n_sub = u_ref.shape[0] // sub

    def matmul(r):
        return jnp.dot(u_ref[r * sub:(r + 1) * sub, :], w_ref[...], preferred_element_type=F32)

    y_next = matmul(0)
    for r in range(n_sub):
        rows = slice(r * sub, (r + 1) * sub)
        y = y_next
        if r + 1 < n_sub:
            y_next = matmul(r + 1)
        o_ref[rows, :] = DEEPNORM_ALPHA * x_ref[rows, :] + gate_ref[0] * y
        mu = jnp.mean(o_ref[rows, :], axis=-1, keepdims=True)
        var = jnp.mean(jnp.square(o_ref[rows, :] - mu), axis=-1, keepdims=True)
        xn = (o_ref[rows, :] - mu) * lax.rsqrt(var + LN_EPS) * lng_ref[...] + lnb_ref[...]
        o_ref[rows, :] = xn
        if emit_h:
            h_ref[rows, :] = (xn * (1.0 + nscale_ref[0]) + nshift_ref[0]).astype(h_ref.dtype)


def _outproj_ln(u2d, w_bf16, x2d, gate, ln_g, ln_b, next_mod=None, *, seq, tm, sub=256):
    m, kdim = u2d.shape
    d = w_bf16.shape[1]
    assert seq % tm == 0 and tm % sub == 0
    tiles_per_seq = seq // tm
    row = pl.BlockSpec((1, d), lambda i: (0, 0))
    mod_spec = pl.BlockSpec((1, 1, d), lambda i: (i // tiles_per_seq, 0, 0))
    tile = pl.BlockSpec((tm, d), lambda i: (i, 0))
    emit_h = next_mod is not None
    in_specs = [
        pl.BlockSpec((tm, kdim), lambda i: (i, 0)),
        pl.BlockSpec((kdim, d), lambda i: (0, 0), pipeline_mode=pl.Buffered(1)),
        tile, mod_spec, row, row,
    ]
    args = [u2d, w_bf16, x2d, gate, ln_g.reshape(1, d), ln_b.reshape(1, d)]
    out_specs, out_shape = tile, jax.ShapeDtypeStruct((m, d), F32)
    if emit_h:
        in_specs += [mod_spec, mod_spec]
        args += list(next_mod)
        out_specs = [tile, tile]
        out_shape = [out_shape, jax.ShapeDtypeStruct((m, d), BF16)]
    return pl.pallas_call(
        functools.partial(_outproj_ln_kernel, sub=sub, emit_h=emit_h),
        grid=(m // tm,),
        in_specs=in_specs,
        out_specs=out_specs,
        out_shape=out_shape,
        compiler_params=_params(("parallel",)),
        name="outproj_ln",
    )(*args)


def kernel(x, c, positions, ret_w_in, ret_w_out, moba_w_in, moba_w_out, w_ada, b_ada, ln_g, ln_b):
    b, s, d = x.shape
    assert DEPTH == 2 and w_ada.shape[0] == DEPTH
    ret_dk = d // RET_HEADS
    ret_dv = 2 * d // RET_HEADS
    moba_dh = d // MOBA_HEADS
    moba_half = moba_dh // 8

    mod = _ada(c, w_ada, b_ada)
    shift, scale, gate = [[mod[l, :, k * d:(k + 1) * d].reshape(b, 1, d) for l in range(DEPTH)]
                          for k in range(3)]
    cos_r, sin_r, mc, ms = _rope_tables(positions, ret_dk, 2 * moba_half)

    x2d = x.reshape(b * s, d)
    n_qk = 2 * RET_HEADS * ret_dk
    qk, h2d = _inproj_qk(x2d, shift[0], scale[0], ret_w_in[0, :, :n_qk].astype(BF16),
                         cos_r.reshape(b * s, LANES), sin_r.reshape(b * s, LANES),
                         seq=s, head_w=ret_dk, k_mult=ret_dk ** -0.5)
    vg = _inproj_plain(h2d, ret_w_in[0, :, n_qk:].astype(BF16))
    u = _retention_core(qk.reshape(b, s, -1), vg.reshape(b, s, -1),
                        heads=RET_HEADS, dk=ret_dk, dv=ret_dv)
    x2d, h2d = _outproj_ln(u.reshape(b * s, -1), ret_w_out[0].astype(BF16), x2d, gate[0],
                           ln_g[0], ln_b[0], next_mod=(shift[1], scale[1]), seq=s, tm=256)

    proj = _inproj_plain(h2d, _moba_w_in_bf16(moba_w_in[0], 2 * MOBA_HEADS, moba_dh, moba_half))
    u = _moba_core(proj.reshape(b, s, -1), mc, ms, heads=MOBA_HEADS, dh=moba_dh)
    x2d = _outproj_ln(u.reshape(b * s, -1), moba_w_out[0].astype(BF16), x2d, gate[1],
                      ln_g[1], ln_b[1], seq=s, tm=512)
    return x2d.reshape(b, s, d)
```

```python
import functools
import math

import jax
import jax.numpy as jnp
import numpy as np
from jax import lax
from jax.experimental import pallas as pl
from jax.experimental.pallas import tpu as pltpu

F32 = jnp.float32
BF16 = jnp.bfloat16

DEPTH = 2
N_MIXERS = 2
DEEPNORM_ALPHA = (2.0 * DEPTH) ** 0.25
LN_EPS = 1e-5

RET_HEADS = 8
RET_CHUNK = 256
RET_THETA = 10000.0
RET_GN_EPS = 1e-5

MOBA_HEADS = 16
MOBA_BLOCK = 256
MOBA_TOPK = 3
ROPE_THETA = 500000.0
NEG = -1e30

LANES = 128
BF16_SUBLANES = 16
VMEM_LIMIT = 56 * 1024 * 1024


def _params(sem):
    return pltpu.CompilerParams(dimension_semantics=sem, vmem_limit_bytes=VMEM_LIMIT)


def _ada_kernel(c_ref, w_ref, b_ref, o_ref):
    acc = jnp.dot(c_ref[...].astype(BF16), w_ref[...].astype(BF16),
                  preferred_element_type=F32)
    o_ref[...] = acc + b_ref[...]


def _ada(c, w_ada, b_ada, tn=512):
    depth, d, n = w_ada.shape
    b = c.shape[0]
    return pl.pallas_call(
        _ada_kernel,
        grid=(depth, n // tn),
        in_specs=[
            pl.BlockSpec((b, d), lambda l, j: (0, 0)),
            pl.BlockSpec((None, d, tn), lambda l, j: (l, 0, j)),
            pl.BlockSpec((None, 1, tn), lambda l, j: (l, 0, j)),
        ],
        out_specs=pl.BlockSpec((None, b, tn), lambda l, j: (l, 0, j)),
        out_shape=jax.ShapeDtypeStruct((depth, b, n), F32),
        compiler_params=_params(("parallel", "parallel")),
        name="ada_mod",
    )(c, w_ada, b_ada.reshape(depth, 1, n))


def _rope_tables_kernel(pos_ref, inv_r_ref, inv_m_ref, cos_r, sin_r, mc, ms, *, half_m):
    pos = pos_ref[0].astype(F32)
    ang = pos * inv_r_ref[...]
    cos_r[0] = jnp.cos(ang)
    sin_r[0] = jnp.sin(ang)
    angm = pos * inv_m_ref[...]
    cm = jnp.cos(angm)
    sm = jnp.sin(angm)
    lane = lax.broadcasted_iota(jnp.int32, angm.shape, 1)
    first = lane < half_m
    second = (lane >= LANES // 2) & (lane < LANES // 2 + half_m)
    mc[0] = jnp.where(first | second, cm, 1.0)
    ms[0] = jnp.where(first, -sm, jnp.where(second, sm, 0.0))


def _rope_tables(positions, ret_dk, moba_rot, ts=512):
    b, s = positions.shape
    half_r = ret_dk // 2
    half_m = moba_rot // 2
    assert half_r == LANES and LANES % half_m == 0
    inv_r = RET_THETA ** (-jnp.arange(half_r, dtype=F32) * 2.0 / ret_dk)
    inv_m = ROPE_THETA ** (-jnp.arange(half_m, dtype=F32) * 2.0 / moba_rot)
    inv_m = jnp.tile(inv_m, LANES // half_m)
    tab = jax.ShapeDtypeStruct((b, s, LANES), F32)
    spec = pl.BlockSpec((1, ts, LANES), lambda i, j: (i, j, 0))
    row = pl.BlockSpec((1, LANES), lambda i, j: (0, 0))
    return pl.pallas_call(
        functools.partial(_rope_tables_kernel, half_m=half_m),
        grid=(b, s // ts),
        in_specs=[pl.BlockSpec((1, ts, 1), lambda i, j: (i, j, 0)), row, row],
        out_specs=[spec] * 4,
        out_shape=[tab] * 4,
        compiler_params=_params(("parallel", "parallel")),
        name="rope_tables",
    )(positions.reshape(b, s, 1), inv_r.reshape(1, LANES), inv_m.reshape(1, LANES))


def _moba_w_prep_kernel(w_ref, o_ref, *, n_qk_tiles, half):
    j = pl.program_id(0)

    @pl.when(j >= n_qk_tiles)
    def _():
        o_ref[...] = w_ref[...].astype(o_ref.dtype)

    @pl.when(j < n_qk_tiles)
    def _():
        mid = LANES // 2
        for hh in range(w_ref.shape[1] // LANES):
            cols = slice(hh * LANES, (hh + 1) * LANES)
            x = w_ref[:, cols]
            lane = lax.broadcasted_iota(jnp.int32, x.shape, 1)
            up = pltpu.roll(x, mid - half, 1)
            dn = pltpu.roll(x, LANES - (mid - half), 1)
            y = jnp.where((lane >= half) & (lane < 2 * half), dn,
                          jnp.where((lane >= mid) & (lane < mid + half), up, x))
            o_ref[:, cols] = y.astype(o_ref.dtype)


def _moba_w_in_bf16(w, n_qk_heads, dh, half, tn=1024):
    d, n = w.shape
    assert dh == LANES and (n_qk_heads * dh) % tn == 0 and n % tn == 0
    spec = pl.BlockSpec((d, tn), lambda j: (0, j))
    return pl.pallas_call(
        functools.partial(_moba_w_prep_kernel, n_qk_tiles=n_qk_heads * dh // tn, half=half),
        grid=(n // tn,),
        in_specs=[spec],
        out_specs=spec,
        out_shape=jax.ShapeDtypeStruct((d, n), BF16),
        compiler_params=_params(("parallel",)),
        name="moba_w_prep",
    )(w)


def _inproj_qk_kernel(x_ref, shift_ref, scale_ref, w_ref, cos_ref, sin_ref, o_ref, h_ref,
                      *, tn, n_q_tiles, head_w, k_mult):
    j = pl.program_id(1)

    @pl.when(j == 0)
    def _():
        h = x_ref[...] * (1.0 + scale_ref[0]) + shift_ref[0]
        h_ref[...] = h.astype(h_ref.dtype)

    acc = jnp.dot(h_ref[...], w_ref[...], preferred_element_type=F32)
    mult = jnp.where(j < n_q_tiles, 1.0, k_mult)
    c = cos_ref[...] * mult
    s = sin_ref[...] * mult
    for hh in range(tn // head_w):
        lo = hh * head_w
        x1 = acc[:, lo:lo + LANES]
        x2 = acc[:, lo + LANES:lo + 2 * LANES]
        o_ref[:, lo:lo + LANES] = (x1 * c - x2 * s).astype(o_ref.dtype)
        o_ref[:, lo + LANES:lo + 2 * LANES] = (x2 * c + x1 * s).astype(o_ref.dtype)


def _inproj_qk(x2d, shift, scale, w_bf16, cos_t, sin_t, *, n, seq, head_w, k_mult,
               tm=1024, tn=1024):
    m, d = x2d.shape
    assert seq % tm == 0 and (n // 2) % tn == 0 and tn % head_w == 0 and head_w == 2 * LANES
    tiles_per_seq = seq // tm
    tab_spec = pl.BlockSpec((tm, LANES), lambda i, j: (i, 0))
    mod_spec = pl.BlockSpec((1, 1, d), lambda i, j: (i // tiles_per_seq, 0, 0))
    return pl.pallas_call(
        functools.partial(_inproj_qk_kernel, tn=tn, n_q_tiles=n // 2 // tn,
                          head_w=head_w, k_mult=k_mult),
        grid=(m // tm, n // tn),
        in_specs=[
            pl.BlockSpec((tm, d), lambda i, j: (i, 0)),
            mod_spec, mod_spec,
            pl.BlockSpec((d, tn), lambda i, j: (0, j)),
            tab_spec, tab_spec,
        ],
        out_specs=[pl.BlockSpec((tm, tn), lambda i, j: (i, j)),
                   pl.BlockSpec((tm, d), lambda i, j: (i, 0))],
        out_shape=[jax.ShapeDtypeStruct((m, n), BF16), jax.ShapeDtypeStruct((m, d), BF16)],
        compiler_params=_params(("parallel", "arbitrary")),
        name="inproj_qk",
    )(x2d, shift, scale, w_bf16, cos_t, sin_t)


def _matmul_kernel(h_ref, w_ref, o_ref):
    o_ref[...] = jnp.dot(h_ref[...], w_ref[...], preferred_element_type=F32).astype(o_ref.dtype)


def _inproj_plain(h2d, w_bf16, *, col0=0, tm=2048, tn=1024):
    m, d = h2d.shape
    n = w_bf16.shape[1] - col0
    assert m % tm == 0 and n % tn == 0 and col0 % tn == 0
    j0 = col0 // tn
    return pl.pallas_call(
        _matmul_kernel,
        grid=(m // tm, n // tn),
        in_specs=[pl.BlockSpec((tm, d), lambda i, j: (i, 0)),
                  pl.BlockSpec((d, tn), lambda i, j: (0, j0 + j))],
        out_specs=pl.BlockSpec((tm, tn), lambda i, j: (i, j)),
        out_shape=jax.ShapeDtypeStruct((m, n), BF16),
        compiler_params=_params(("parallel", "parallel")),
        name="inproj_plain",
    )(h2d, w_bf16)


def _retention_kernel(q_ref, k_ref, v_ref, g_ref, dec_ref, qd_ref, kd_ref, cd_ref, o_ref, st_scr,
                      *, n_chunks, chunk):
    st_scr[...] = jnp.zeros_like(st_scr)
    qd = qd_ref[...]
    cd = cd_ref[...]

    def state_free(ci):
        rows = slice(ci * chunk, (ci + 1) * chunk)
        qc = q_ref[0, rows, :]
        kc = k_ref[0, rows, :]
        vc = v_ref[0, rows, :]
        scores = lax.dot_general(qc, kc, (((1,), (1,)), ((), ())),
                                 preferred_element_type=F32) * dec_ref[...]
        inner = jnp.dot(scores.astype(BF16), vc, preferred_element_type=F32)
        kdec = (kc.astype(F32) * kd_ref[...]).astype(BF16)
        update = lax.dot_general(kdec, vc, (((0,), (0,)), ((), ())), preferred_element_type=F32)
        return inner, update

    def finish(ci, inner, update):
        rows = slice(ci * chunk, (ci + 1) * chunk)
        state = st_scr[...]
        cross = jnp.dot(q_ref[0, rows, :], state.astype(BF16), preferred_element_type=F32) * qd
        st_scr[...] = state * cd + update
        o = inner + cross
        mu = jnp.mean(o, axis=-1, keepdims=True)
        oc = o - mu
        var = jnp.mean(oc * oc, axis=-1, keepdims=True)
        gn = oc * lax.rsqrt(var + RET_GN_EPS)
        g = g_ref[0, rows, :].astype(F32)
        o_ref[0, rows, :] = (gn * (g / (1.0 + jnp.exp(-g)))).astype(o_ref.dtype)

    nxt = state_free(0)
    for ci in range(n_chunks):
        cur = nxt
        if ci + 1 < n_chunks:
            nxt = state_free(ci + 1)
        finish(ci, *cur)


def _retention_decays(heads, chunk):
    log_gamma = jnp.log(1.0 - 2.0 ** (-5.0 - jnp.arange(heads, dtype=F32)))
    n = jnp.arange(chunk, dtype=F32)
    diff = n[:, None] - n[None, :]
    inner = jnp.where(diff >= 0, jnp.exp(jnp.maximum(diff, 0.0) * log_gamma[:, None, None]), 0.0)
    q_decay = jnp.exp((n + 1.0) * log_gamma[:, None])
    k_decay = jnp.exp((chunk - 1.0 - n) * log_gamma[:, None])
    chunk_decay = jnp.exp(chunk * log_gamma)
    return (inner.astype(F32), q_decay.reshape(heads, chunk, 1), k_decay.reshape(heads, chunk, 1),
            chunk_decay.reshape(heads, 1, 1))


def _retention_core(qk, vg, *, heads, dk, dv):
    b, s, _ = qk.shape
    chunk = RET_CHUNK
    assert s % chunk == 0
    dec, qd, kd, cd = _retention_decays(heads, chunk)
    return pl.pallas_call(
        functools.partial(_retention_kernel, n_chunks=s // chunk, chunk=chunk),
        grid=(b, heads),
        in_specs=[
            pl.BlockSpec((1, s, dk), lambda i, h: (i, 0, h)),
            pl.BlockSpec((1, s, dk), lambda i, h: (i, 0, heads + h)),
            pl.BlockSpec((1, s, dv), lambda i, h: (i, 0, h)),
            pl.BlockSpec((1, s, dv), lambda i, h: (i, 0, heads + h)),
            pl.BlockSpec((None, chunk, chunk), lambda i, h: (h, 0, 0)),
            pl.BlockSpec((None, chunk, 1), lambda i, h: (h, 0, 0)),
            pl.BlockSpec((None, chunk, 1), lambda i, h: (h, 0, 0)),
            pl.BlockSpec((None, 1, 1), lambda i, h: (h, 0, 0)),
        ],
        out_specs=pl.BlockSpec((1, s, dv), lambda i, h: (i, 0, h)),
        out_shape=jax.ShapeDtypeStruct((b, s, heads * dv), BF16),
        scratch_shapes=[pltpu.VMEM((dk, dv), F32)],
        compiler_params=_params(("parallel", "parallel")),
        name="retention_core",
    )(qk, qk, vg, vg, dec, qd, kd, cd)


def _moba_kernel(q_ref, k_ref, v_ref, g_ref, mc_ref, ms_ref, o_ref,
                 qr_scr, kr_scr, vt_scr, km_scr, *s_scrs, n_blocks, blk, topk, exp_scale):
    dh = q_ref.shape[2]
    vt_scr[dh:, :] = jnp.ones((vt_scr.shape[0] - dh, vt_scr.shape[1]), BF16)
    km_scr[...] = jnp.zeros_like(km_scr)

    def prepare(b):
        rows = slice(b * blk, (b + 1) * blk)
        mc = mc_ref[0, rows, :]
        ms = ms_ref[0, rows, :]

        def rotary(x_ref):
            x = x_ref[0, rows, :].astype(F32)
            return x * mc + pltpu.roll(x, LANES // 2, 1) * ms

        qr_scr[rows, :] = (rotary(q_ref) * exp_scale).astype(BF16)
        kr = rotary(k_ref)
        kr_scr[rows, :] = kr.astype(BF16)
        km_scr[b:b + 1, :] = jnp.mean(kr, axis=0, keepdims=True)
        vt_scr[:dh, rows] = v_ref[0, rows, :].T

    nt = (((1,), (1,)), ((), ()))
    key_id = lax.broadcasted_iota(jnp.int32, (blk, blk), 0)
    qry_id = lax.broadcasted_iota(jnp.int32, (blk, blk), 1)
    blk_id = lax.broadcasted_iota(jnp.int32, (n_blocks, blk), 0)

    def scores(i):
        cols = slice(i * blk, (i + 1) * blk)
        s_scr = s_scrs[i]
        qi = qr_scr[cols, :]
        ranked = i > topk
        if ranked:
            gi = lax.dot_general(km_scr[...].astype(BF16), qi, nt,
                                 preferred_element_type=F32)
            past = blk_id < i
        m = None
        bias = []
        for j in range(i + 1):
            sj = lax.dot_general(kr_scr[j * blk:(j + 1) * blk, :], qi, nt,
                                 preferred_element_type=F32)
            if j == i:
                sj = jnp.where(key_id <= qry_id, sj, NEG)
            s_scr[j * blk:(j + 1) * blk, :] = sj
            mj = jnp.max(sj, axis=0, keepdims=True)
            if ranked and j < i:
                gj = gi[j:j + 1, :]
                beats = past & ((gi > gj) | ((gi == gj) & (blk_id < j)))
                cnt = jnp.sum(jnp.where(beats, 1.0, 0.0), axis=0, keepdims=True)
                bias.append(jnp.where(cnt < topk, 0.0, NEG))
                mj = mj + bias[-1]
            m = mj if m is None else jnp.maximum(m, mj)
        if not ranked:
            return [m]
        return [m - bj for bj in bias] + [m]

    def attend(i, subs):
        cols = slice(i * blk, (i + 1) * blk)
        ot = None
        for j in range(i + 1):
            keys = slice(j * blk, (j + 1) * blk)
            e = jnp.exp2(s_scrs[i][keys, :] - subs[min(j, len(subs) - 1)]).astype(BF16)
            pj = jnp.dot(vt_scr[:, keys], e, preferred_element_type=F32)
            ot = pj if ot is None else pj + ot
        o = (ot[:dh, :] * (1.0 / ot[dh:dh + 1, :])).T
        g = g_ref[0, cols, :].astype(F32)
        o_ref[0, cols, :] = (o * (g / (1.0 + jnp.exp(-g)))).astype(o_ref.dtype)

    order = list(range(1, n_blocks, 2)) + list(range((n_blocks - 1) // 2 * 2, -1, -2))
    prepared = 0

    def prepare_upto(b):
        nonlocal prepared
        for blk_i in range(prepared, b + 1):
            prepare(blk_i)
        prepared = max(prepared, b + 1)

    prepare_upto(order[0])
    nxt = scores(order[0])
    for pos, i in enumerate(order):
        cur = nxt
        if pos + 2 < n_blocks:
            prepare_upto(order[pos + 2])
        if pos + 1 < n_blocks:
            prepare_upto(order[pos + 1])
            nxt = scores(order[pos + 1])
        attend(i, cur)


def _moba_core(proj, mc, ms, *, heads, dh):
    b, s, _ = proj.shape
    blk = MOBA_BLOCK
    assert s % blk == 0 and dh == LANES
    n_blocks = s // blk
    spec = lambda off: pl.BlockSpec((1, s, dh), lambda i, h: (i, 0, off + h))
    tab = pl.BlockSpec((1, s, dh), lambda i, h: (i, 0, 0))
    return pl.pallas_call(
        functools.partial(_moba_kernel, n_blocks=n_blocks, blk=blk,
                          topk=min(MOBA_TOPK, n_blocks),
                          exp_scale=dh ** -0.5 * math.log2(math.e)),
        grid=(b, heads),
        in_specs=[spec(0), spec(heads), spec(2 * heads), spec(3 * heads), tab, tab],
        out_specs=spec(0),
        out_shape=jax.ShapeDtypeStruct((b, s, heads * dh), BF16),
        scratch_shapes=[pltpu.VMEM((s, dh), BF16), pltpu.VMEM((s, dh), BF16),
                        pltpu.VMEM((dh + BF16_SUBLANES, s), BF16),
                        pltpu.VMEM((n_blocks, dh), F32)]
        + [pltpu.VMEM(((i + 1) * blk, blk), F32) for i in range(n_blocks)],
        compiler_params=_params(("parallel", "parallel")),
        name="moba_core",
    )(proj, proj, proj, proj, mc, ms)


def _outproj_ln_kernel(u_ref, w_ref, x_ref, gate_ref, lng_ref, lnb_ref, *rest, sub, emit_h):
    if emit_h:
        nshift_ref, nscale_ref, o_ref, h_ref = rest
    else:
        (o_ref,) = rest
    n_sub = u_ref.shape[0] // sub

    def matmul(r):
        return jnp.dot(u_ref[r * sub:(r + 1) * sub, :], w_ref[...], preferred_element_type=F32)

    y_next = matmul(0)
    for r in range(n_sub):
        rows = slice(r * sub, (r + 1) * sub)
        y = y_next
        if r + 1 < n_sub:
            y_next = matmul(r + 1)
        o_ref[rows, :] = DEEPNORM_ALPHA * x_ref[rows, :] + gate_ref[0] * y
        mu = jnp.mean(o_ref[rows, :], axis=-1, keepdims=True)
        var = jnp.mean(jnp.square(o_ref[rows, :] - mu), axis=-1, keepdims=True)
        xn = (o_ref[rows, :] - mu) * lax.rsqrt(var + LN_EPS) * lng_ref[...] + lnb_ref[...]
        o_ref[rows, :] = xn
        if emit_h:
            h_ref[rows, :] = (xn * (1.0 + nscale_ref[0]) + nshift_ref[0]).astype(h_ref.dtype)


def _outproj_ln(u2d, w_bf16, x2d, gate, ln_g, ln_b, next_mod=None, *, seq, tm, sub=256):
    m, kdim = u2d.shape
    d = w_bf16.shape[1]
    assert seq % tm == 0 and tm % sub == 0
    tiles_per_seq = seq // tm
    row = pl.BlockSpec((1, d), lambda i: (0, 0))
    mod_spec = pl.BlockSpec((1, 1, d), lambda i: (i // tiles_per_seq, 0, 0))
    tile = pl.BlockSpec((tm, d), lambda i: (i, 0))
    emit_h = next_mod is not None
    in_specs = [
        pl.BlockSpec((tm, kdim), lambda i: (i, 0)),
        pl.BlockSpec((kdim, d), lambda i: (0, 0), pipeline_mode=pl.Buffered(1)),
        tile, mod_spec, row, row,
    ]
    args = [u2d, w_bf16, x2d, gate, ln_g.reshape(1, d), ln_b.reshape(1, d)]
    out_specs, out_shape = tile, jax.ShapeDtypeStruct((m, d), F32)
    if emit_h:
        in_specs += [mod_spec, mod_spec]
        args += list(next_mod)
        out_specs = [tile, tile]
        out_shape = [out_shape, jax.ShapeDtypeStruct((m, d), BF16)]
    return pl.pallas_call(
        functools.partial(_outproj_ln_kernel, sub=sub, emit_h=emit_h),
        grid=(m // tm,),
        in_specs=in_specs,
        out_specs=out_specs,
        out_shape=out_shape,
        compiler_params=_params(("parallel",)),
        name="outproj_ln",
    )(*args)


def kernel(x, c, positions, ret_w_in, ret_w_out, moba_w_in, moba_w_out, w_ada, b_ada, ln_g, ln_b):
    b, s, d = x.shape
    assert DEPTH == 2 and w_ada.shape[0] == DEPTH
    ret_dk = d // RET_HEADS
    ret_dv = 2 * d // RET_HEADS
    moba_dh = d // MOBA_HEADS
    moba_half = moba_dh // 8

    mod = _ada(c, w_ada, b_ada)
    shift, scale, gate = [[mod[l, :, k * d:(k + 1) * d].reshape(b, 1, d) for l in range(DEPTH)]
                          for k in range(3)]
    cos_r, sin_r, mc, ms = _rope_tables(positions, ret_dk, 2 * moba_half)

    x2d = x.reshape(b * s, d)
    n_qk = 2 * RET_HEADS * ret_dk
    w_in = ret_w_in[0].astype(BF16)
    qk, h2d = _inproj_qk(x2d, shift[0], scale[0], w_in,
                         cos_r.reshape(b * s, LANES), sin_r.reshape(b * s, LANES),
                         n=n_qk, seq=s, head_w=ret_dk, k_mult=ret_dk ** -0.5)
    vg = _inproj_plain(h2d, w_in, col0=n_qk)
    u = _retention_core(qk.reshape(b, s, -1), vg.reshape(b, s, -1),
                        heads=RET_HEADS, dk=ret_dk, dv=ret_dv)
    x2d, h2d = _outproj_ln(u.reshape(b * s, -1), ret_w_out[0].astype(BF16), x2d, gate[0],
                           ln_g[0], ln_b[0], next_mod=(shift[1], scale[1]), seq=s, tm=256)

    proj = _inproj_plain(h2d, _moba_w_in_bf16(moba_w_in[0], 2 * MOBA_HEADS, moba_dh, moba_half))
    u = _moba_core(proj.reshape(b, s, -1), mc, ms, heads=MOBA_HEADS, dh=moba_dh)
    x2d = _outproj_ln(u.reshape(b * s, -1), moba_w_out[0].astype(BF16), x2d, gate[1],
                      ln_g[1], ln_b[1], seq=s, tm=512)
    return x2d.reshape(b, s, d)
```

```python
import functools
import math

import jax
import jax.numpy as jnp
import numpy as np
from jax import lax
from jax.experimental import pallas as pl
from jax.experimental.pallas import tpu as pltpu

F32 = jnp.float32
BF16 = jnp.bfloat16

DEPTH = 2
N_MIXERS = 2
DEEPNORM_ALPHA = (2.0 * DEPTH) ** 0.25
LN_EPS = 1e-5

RET_HEADS = 8
RET_CHUNK = 256
RET_THETA = 10000.0
RET_GN_EPS = 1e-5

MOBA_HEADS = 16
MOBA_BLOCK = 256
MOBA_TOPK = 3
MOBA_SCORE_ROWS = 256
ROPE_THETA = 500000.0
NEG = -1e30

LANES = 128
BF16_SUBLANES = 16
VMEM_LIMIT = 56 * 1024 * 1024


def _params(sem):
    return pltpu.CompilerParams(dimension_semantics=sem, vmem_limit_bytes=VMEM_LIMIT)


def _ada_kernel(c_ref, w_ref, b_ref, o_ref):
    acc = jnp.dot(c_ref[...].astype(BF16), w_ref[...].astype(BF16),
                  preferred_element_type=F32)
    o_ref[...] = acc + b_ref[...]


def _ada(c, w_ada, b_ada, tn=512):
    depth, d, n = w_ada.shape
    b = c.shape[0]
    return pl.pallas_call(
        _ada_kernel,
        grid=(depth, n // tn),
        in_specs=[
            pl.BlockSpec((b, d), lambda l, j: (0, 0)),
            pl.BlockSpec((None, d, tn), lambda l, j: (l, 0, j)),
            pl.BlockSpec((None, 1, tn), lambda l, j: (l, 0, j)),
        ],
        out_specs=pl.BlockSpec((None, b, tn), lambda l, j: (l, 0, j)),
        out_shape=jax.ShapeDtypeStruct((depth, b, n), F32),
        compiler_params=_params(("parallel", "parallel")),
        name="ada_mod",
    )(c, w_ada, b_ada.reshape(depth, 1, n))


def _rope_tables_kernel(pos_ref, inv_r_ref, inv_m_ref, cos_r, sin_r, mc, ms, *, half_m):
    pos = pos_ref[0].astype(F32)
    ang = pos * inv_r_ref[...]
    cos_r[0] = jnp.cos(ang)
    sin_r[0] = jnp.sin(ang)
    angm = pos * inv_m_ref[...]
    cm = jnp.cos(angm)
    sm = jnp.sin(angm)
    lane = lax.broadcasted_iota(jnp.int32, angm.shape, 1)
    first = lane < half_m
    second = (lane >= LANES // 2) & (lane < LANES // 2 + half_m)
    mc[0] = jnp.where(first | second, cm, 1.0)
    ms[0] = jnp.where(first, -sm, jnp.where(second, sm, 0.0))


def _rope_tables(positions, ret_dk, moba_rot, ts=512):
    b, s = positions.shape
    half_r = ret_dk // 2
    half_m = moba_rot // 2
    assert half_r == LANES and LANES % half_m == 0
    inv_r = RET_THETA ** (-jnp.arange(half_r, dtype=F32) * 2.0 / ret_dk)
    inv_m = ROPE_THETA ** (-jnp.arange(half_m, dtype=F32) * 2.0 / moba_rot)
    inv_m = jnp.tile(inv_m, LANES // half_m)
    tab = jax.ShapeDtypeStruct((b, s, LANES), F32)
    spec = pl.BlockSpec((1, ts, LANES), lambda i, j: (i, j, 0))
    row = pl.BlockSpec((1, LANES), lambda i, j: (0, 0))
    return pl.pallas_call(
        functools.partial(_rope_tables_kernel, half_m=half_m),
        grid=(b, s // ts),
        in_specs=[pl.BlockSpec((1, ts, 1), lambda i, j: (i, j, 0)), row, row],
        out_specs=[spec] * 4,
        out_shape=[tab] * 4,
        compiler_params=_params(("parallel", "parallel")),
        name="rope_tables",
    )(positions.reshape(b, s, 1), inv_r.reshape(1, LANES), inv_m.reshape(1, LANES))


def _moba_w_prep_kernel(w_ref, o_ref, *, n_qk_tiles, half):
    j = pl.program_id(0)

    @pl.when(j >= n_qk_tiles)
    def _():
        o_ref[...] = w_ref[...].astype(o_ref.dtype)

    @pl.when(j < n_qk_tiles)
    def _():
        mid = LANES // 2
        for hh in range(w_ref.shape[1] // LANES):
            cols = slice(hh * LANES, (hh + 1) * LANES)
            x = w_ref[:, cols]
            lane = lax.broadcasted_iota(jnp.int32, x.shape, 1)
            up = pltpu.roll(x, mid - half, 1)
            dn = pltpu.roll(x, LANES - (mid - half), 1)
            y = jnp.where((lane >= half) & (lane < 2 * half), dn,
                          jnp.where((lane >= mid) & (lane < mid + half), up, x))
            o_ref[:, cols] = y.astype(o_ref.dtype)


def _moba_w_in_bf16(w, n_qk_heads, dh, half, tn=1024):
    d, n = w.shape
    assert dh == LANES and (n_qk_heads * dh) % tn == 0 and n % tn == 0
    spec = pl.BlockSpec((d, tn), lambda j: (0, j))
    return pl.pallas_call(
        functools.partial(_moba_w_prep_kernel, n_qk_tiles=n_qk_heads * dh // tn, half=half),
        grid=(n // tn,),
        in_specs=[spec],
        out_specs=spec,
        out_shape=jax.ShapeDtypeStruct((d, n), BF16),
        compiler_params=_params(("parallel",)),
        name="moba_w_prep",
    )(w)


def _inproj_qk_kernel(x_ref, shift_ref, scale_ref, w_ref, cos_ref, sin_ref, o_ref, h_ref,
                      *, tn, n_q_tiles, head_w, k_mult):
    j = pl.program_id(1)

    @pl.when(j == 0)
    def _():
        h = x_ref[...] * (1.0 + scale_ref[0]) + shift_ref[0]
        h_ref[...] = h.astype(h_ref.dtype)

    acc = jnp.dot(h_ref[...], w_ref[...], preferred_element_type=F32)
    mult = jnp.where(j < n_q_tiles, 1.0, k_mult)
    c = cos_ref[...] * mult
    s = sin_ref[...] * mult
    for hh in range(tn // head_w):
        lo = hh * head_w
        x1 = acc[:, lo:lo + LANES]
        x2 = acc[:, lo + LANES:lo + 2 * LANES]
        o_ref[:, lo:lo + LANES] = (x1 * c - x2 * s).astype(o_ref.dtype)
        o_ref[:, lo + LANES:lo + 2 * LANES] = (x2 * c + x1 * s).astype(o_ref.dtype)


def _inproj_qk(x2d, shift, scale, w_bf16, cos_t, sin_t, *, n, seq, head_w, k_mult,
               tm=1024, tn=1024):
    m, d = x2d.shape
    assert seq % tm == 0 and (n // 2) % tn == 0 and tn % head_w == 0 and head_w == 2 * LANES
    tiles_per_seq = seq // tm
    tab_spec = pl.BlockSpec((tm, LANES), lambda i, j: (i, 0))
    mod_spec = pl.BlockSpec((1, 1, d), lambda i, j: (i // tiles_per_seq, 0, 0))
    return pl.pallas_call(
        functools.partial(_inproj_qk_kernel, tn=tn, n_q_tiles=n // 2 // tn,
                          head_w=head_w, k_mult=k_mult),
        grid=(m // tm, n // tn),
        in_specs=[
            pl.BlockSpec((tm, d), lambda i, j: (i, 0)),
            mod_spec, mod_spec,
            pl.BlockSpec((d, tn), lambda i, j: (0, j)),
            tab_spec, tab_spec,
        ],
        out_specs=[pl.BlockSpec((tm, tn), lambda i, j: (i, j)),
                   pl.BlockSpec((tm, d), lambda i, j: (i, 0))],
        out_shape=[jax.ShapeDtypeStruct((m, n), BF16), jax.ShapeDtypeStruct((m, d), BF16)],
        compiler_params=_params(("parallel", "arbitrary")),
        name="inproj_qk",
    )(x2d, shift, scale, w_bf16, cos_t, sin_t)


def _matmul_kernel(h_ref, w_ref, o_ref):
    o_ref[...] = jnp.dot(h_ref[...], w_ref[...], preferred_element_type=F32).astype(o_ref.dtype)


def _inproj_plain(h2d, w_bf16, *, col0=0, tm=2048, tn=1024):
    m, d = h2d.shape
    n = w_bf16.shape[1] - col0
    assert m % tm == 0 and n % tn == 0 and col0 % tn == 0
    j0 = col0 // tn
    return pl.pallas_call(
        _matmul_kernel,
        grid=(m // tm, n // tn),
        in_specs=[pl.BlockSpec((tm, d), lambda i, j: (i, 0)),
                  pl.BlockSpec((d, tn), lambda i, j: (0, j0 + j))],
        out_specs=pl.BlockSpec((tm, tn), lambda i, j: (i, j)),
        out_shape=jax.ShapeDtypeStruct((m, n), BF16),
        compiler_params=_params(("parallel", "parallel")),
        name="inproj_plain",
    )(h2d, w_bf16)


def _silu(g):
    half = 0.5 * g
    return half + half * jnp.tanh(half)


def _retention_kernel(q_ref, k_ref, v_ref, g_ref, dec_ref, qd_ref, kd_ref, cd_ref, o_ref, st_scrs,
                      *, n_chunks, chunk, dk, dv):
    n_heads = q_ref.shape[2] // dk
    programs = [_retention_head_program(hh, q_ref, k_ref, v_ref, g_ref, dec_ref.at[hh],
                                        qd_ref.at[hh], kd_ref.at[hh], cd_ref.at[hh], o_ref,
                                        st_scrs.at[hh], n_chunks=n_chunks, chunk=chunk,
                                        dk=dk, dv=dv)
                for hh in range(n_heads)]
    while programs:
        programs = [p for p in programs if next(p, "done") != "done"]


def _retention_head_program(hh, q_ref, k_ref, v_ref, g_ref, dec_ref, qd_ref, kd_ref, cd_ref,
                            o_ref, st_scr, *, n_chunks, chunk, dk, dv):
    qk_cols = slice(hh * dk, (hh + 1) * dk)
    vg_cols = slice(hh * dv, (hh + 1) * dv)
    st_scr[...] = jnp.zeros_like(st_scr)
    qd = qd_ref[...]
    cd = cd_ref[...]

    def state_free(ci):
        rows = slice(ci * chunk, (ci + 1) * chunk)
        qc = q_ref[0, rows, qk_cols]
        kc = k_ref[0, rows, qk_cols]
        vc = v_ref[0, rows, vg_cols]
        scores = lax.dot_general(qc, kc, (((1,), (1,)), ((), ())),
                                 preferred_element_type=F32) * dec_ref[...]
        inner = jnp.dot(scores.astype(BF16), vc, preferred_element_type=F32)
        kdec = (kc.astype(F32) * kd_ref[...]).astype(BF16)
        update = lax.dot_general(kdec, vc, (((0,), (0,)), ((), ())), preferred_element_type=F32)
        return inner, update

    def finish(ci, inner, update):
        rows = slice(ci * chunk, (ci + 1) * chunk)
        state = st_scr[...]
        cross = jnp.dot(q_ref[0, rows, qk_cols], state.astype(BF16),
                        preferred_element_type=F32) * qd
        st_scr[...] = state * cd + update
        o = inner + cross
        mu = jnp.mean(o, axis=-1, keepdims=True)
        oc = o - mu
        var = jnp.mean(oc * oc, axis=-1, keepdims=True)
        gn = oc * lax.rsqrt(var + RET_GN_EPS)
        g = g_ref[0, rows, vg_cols].astype(F32)
        o_ref[0, rows, vg_cols] = (gn * _silu(g)).astype(o_ref.dtype)

    nxt = state_free(0)
    yield
    for ci in range(n_chunks):
        cur = nxt
        if ci + 1 < n_chunks:
            nxt = state_free(ci + 1)
            yield
        finish(ci, *cur)
        yield


def _retention_decays(heads, chunk):
    log_gamma = jnp.log(1.0 - 2.0 ** (-5.0 - jnp.arange(heads, dtype=F32)))
    n = jnp.arange(chunk, dtype=F32)
    diff = n[:, None] - n[None, :]
    inner = jnp.where(diff >= 0, jnp.exp(jnp.maximum(diff, 0.0) * log_gamma[:, None, None]), 0.0)
    q_decay = jnp.exp((n + 1.0) * log_gamma[:, None])
    k_decay = jnp.exp((chunk - 1.0 - n) * log_gamma[:, None])
    chunk_decay = jnp.exp(chunk * log_gamma)
    return (inner.astype(F32), q_decay.reshape(heads, chunk, 1), k_decay.reshape(heads, chunk, 1),
            chunk_decay.reshape(heads, 1, 1))


def _retention_core(qk, vg, *, heads, dk, dv, heads_per_step=2):
    b, s, _ = qk.shape
    chunk = RET_CHUNK
    hp = heads_per_step
    assert s % chunk == 0 and heads % hp == 0
    groups = heads // hp
    dec, qd, kd, cd = _retention_decays(heads, chunk)
    return pl.pallas_call(
        functools.partial(_retention_kernel, n_chunks=s // chunk, chunk=chunk, dk=dk, dv=dv),
        grid=(b, groups),
        in_specs=[
            pl.BlockSpec((1, s, hp * dk), lambda i, h: (i, 0, h)),
            pl.BlockSpec((1, s, hp * dk), lambda i, h: (i, 0, groups + h)),
            pl.BlockSpec((1, s, hp * dv), lambda i, h: (i, 0, h)),
            pl.BlockSpec((1, s, hp * dv), lambda i, h: (i, 0, groups + h)),
            pl.BlockSpec((hp, chunk, chunk), lambda i, h: (h, 0, 0)),
            pl.BlockSpec((hp, chunk, 1), lambda i, h: (h, 0, 0)),
            pl.BlockSpec((hp, chunk, 1), lambda i, h: (h, 0, 0)),
            pl.BlockSpec((hp, 1, 1), lambda i, h: (h, 0, 0)),
        ],
        out_specs=pl.BlockSpec((1, s, hp * dv), lambda i, h: (i, 0, h)),
        out_shape=jax.ShapeDtypeStruct((b, s, heads * dv), BF16),
        scratch_shapes=[pltpu.VMEM((hp, dk, dv), F32)],
        compiler_params=_params(("parallel", "parallel")),
        name="retention_core",
    )(qk, qk, vg, vg, dec, qd, kd, cd)


def _moba_kernel(q_ref, k_ref, v_ref, g_ref, mc_ref, ms_ref, o_ref,
                 qr_scrs, kr_scrs, vt_scrs, km_scrs, *s_scrs,
                 n_blocks, blk, sub, topk, exp_scale, dh):
    n_heads = q_ref.shape[2] // dh
    programs = [_moba_head_program(hh, q_ref, k_ref, v_ref, g_ref, mc_ref, ms_ref, o_ref,
                                   qr_scrs.at[hh], kr_scrs.at[hh], vt_scrs.at[hh], km_scrs.at[hh],
                                   s_scrs[hh * n_blocks:(hh + 1) * n_blocks],
                                   n_blocks=n_blocks, blk=blk, sub=sub, topk=topk,
                                   exp_scale=exp_scale, dh=dh)
                for hh in range(n_heads)]
    while programs:
        programs = [p for p in programs if next(p, "done") != "done"]


def _moba_head_program(hh, q_ref, k_ref, v_ref, g_ref, mc_ref, ms_ref, o_ref,
                       qr_scr, kr_scr, vt_scr, km_scr, s_scrs, *,
                       n_blocks, blk, sub, topk, exp_scale, dh):
    lanes = slice(hh * dh, (hh + 1) * dh)
    vt_scr[dh:, :] = jnp.ones((vt_scr.shape[0] - dh, vt_scr.shape[1]), BF16)
    km_scr[...] = jnp.zeros_like(km_scr)

    def prepare(b):
        rows = slice(b * blk, (b + 1) * blk)
        mc = mc_ref[0, rows, :]
        ms = ms_ref[0, rows, :]

        def rotary(x_ref):
            x = x_ref[0, rows, lanes].astype(F32)
            return x * mc + pltpu.roll(x, LANES // 2, 1) * ms

        qr_scr[rows, :] = (rotary(q_ref) * exp_scale).astype(BF16)
        kr = rotary(k_ref)
        kr_scr[rows, :] = kr.astype(BF16)
        km_scr[b:b + 1, :] = jnp.mean(kr, axis=0, keepdims=True)
        vt_scr[:dh, rows] = v_ref[0, rows, lanes].T

    nt = (((1,), (1,)), ((), ()))
    key_id = lax.broadcasted_iota(jnp.int32, (sub, blk), 0)
    qry_id = lax.broadcasted_iota(jnp.int32, (sub, blk), 1)
    blk_id = lax.broadcasted_iota(jnp.int32, (n_blocks, blk), 0)

    def scores(i):
        cols = slice(i * blk, (i + 1) * blk)
        s_scr = s_scrs[i]
        qi = qr_scr[cols, :]
        ranked = i > topk
        if ranked:
            gi = lax.dot_general(km_scr[...].astype(BF16), qi, nt,
                                 preferred_element_type=F32)
            past = blk_id < i
        m = None
        bias = []
        for j in range(i + 1):
            mj = None
            for part in range(blk // sub):
                keys = slice(j * blk + part * sub, j * blk + (part + 1) * sub)
                sj = lax.dot_general(kr_scr[keys, :], qi, nt,
                                     preferred_element_type=F32)
                if j == i:
                    sj = jnp.where(key_id + part * sub <= qry_id, sj, NEG)
                s_scr[keys, :] = sj
                mp = jnp.max(sj, axis=0, keepdims=True)
                mj = mp if mj is None else jnp.maximum(mj, mp)
            if ranked and j < i:
                gj = gi[j:j + 1, :]
                beats = past & ((gi > gj) | ((gi == gj) & (blk_id < j)))
                cnt = jnp.sum(jnp.where(beats, 1.0, 0.0), axis=0, keepdims=True)
                bias.append(jnp.where(cnt < topk, 0.0, NEG))
                mj = mj + bias[-1]
            m = mj if m is None else jnp.maximum(m, mj)
        if not ranked:
            return [m]
        return [m - bj for bj in bias] + [m]

    def attend(i, subs):
        cols = slice(i * blk, (i + 1) * blk)
        ot = None
        for j in range(i + 1):
            keys = slice(j * blk, (j + 1) * blk)
            e = jnp.exp2(s_scrs[i][keys, :] - subs[min(j, len(subs) - 1)]).astype(BF16)
            pj = jnp.dot(vt_scr[:, keys], e, preferred_element_type=F32)
            ot = pj if ot is None else pj + ot
        o = (ot[:dh, :] * (1.0 / ot[dh:dh + 1, :])).T
        g = g_ref[0, cols, lanes].astype(F32)
        o_ref[0, cols, lanes] = (o * _silu(g)).astype(o_ref.dtype)

    order = list(range(1, n_blocks, 2)) + list(range((n_blocks - 1) // 2 * 2, -1, -2))
    prepared = 0

    def prepare_upto(b):
        nonlocal prepared
        for blk_i in range(prepared, b + 1):
            prepare(blk_i)
        prepared = max(prepared, b + 1)

    prepare_upto(order[0])
    yield
    nxt = scores(order[0])
    yield
    for pos, i in enumerate(order):
        cur = nxt
        if pos + 2 < n_blocks:
            prepare_upto(order[pos + 2])
        if pos + 1 < n_blocks:
            prepare_upto(order[pos + 1])
            nxt = scores(order[pos + 1])
            yield
        attend(i, cur)
        yield


def _moba_core(proj, mc, ms, *, heads, dh, heads_per_step=2):
    b, s, _ = proj.shape
    blk = MOBA_BLOCK
    hp = heads_per_step
    assert s % blk == 0 and dh == LANES and heads % hp == 0
    n_blocks = s // blk
    groups = heads // hp
    spec = lambda off: pl.BlockSpec((1, s, hp * dh), lambda i, h: (i, 0, off + h))
    tab = pl.BlockSpec((1, s, dh), lambda i, h: (i, 0, 0))
    return pl.pallas_call(
        functools.partial(_moba_kernel, n_blocks=n_blocks, blk=blk, sub=MOBA_SCORE_ROWS,
                          topk=min(MOBA_TOPK, n_blocks),
                          exp_scale=dh ** -0.5 * math.log2(math.e), dh=dh),
        grid=(b, groups),
        in_specs=[spec(0), spec(groups), spec(2 * groups), spec(3 * groups), tab, tab],
        out_specs=spec(0),
        out_shape=jax.ShapeDtypeStruct((b, s, heads * dh), BF16),
        scratch_shapes=[pltpu.VMEM((hp, s, dh), BF16), pltpu.VMEM((hp, s, dh), BF16),
                        pltpu.VMEM((hp, dh + BF16_SUBLANES, s), BF16),
                        pltpu.VMEM((hp, n_blocks, dh), F32)]
        + [pltpu.VMEM(((i + 1) * blk, blk), F32) for _ in range(hp) for i in range(n_blocks)],
        compiler_params=_params(("parallel", "parallel")),
        name="moba_core",
    )(proj, proj, proj, proj, mc, ms)


def _outproj_ln_kernel(u_ref, w_ref, x_ref, gate_ref, lng_ref, lnb_ref, *rest, sub, emit_h):
    if emit_h:
        nshift_ref, nscale_ref, o_ref, h_ref = rest
    else:
        (o_ref,) = rest
    n_sub = u_ref.shape[0] // sub

    def matmul(r):
        return jnp.dot(u_ref[r * sub:(r + 1) * sub, :], w_ref[...], preferred_element_type=F32)

    y_next = matmul(0)
    for r in range(n_sub):
        rows = slice(r * sub, (r + 1) * sub)
        y = y_next
        if r + 1 < n_sub:
            y_next = matmul(r + 1)
        o_ref[rows, :] = DEEPNORM_ALPHA * x_ref[rows, :] + gate_ref[0] * y
        mu = jnp.mean(o_ref[rows, :], axis=-1, keepdims=True)
        var = jnp.mean(jnp.square(o_ref[rows, :] - mu), axis=-1, keepdims=True)
        xn = (o_ref[rows, :] - mu) * lax.rsqrt(var + LN_EPS) * lng_ref[...] + lnb_ref[...]
        o_ref[rows, :] = xn
        if emit_h:
            h_ref[rows, :] = (xn * (1.0 + nscale_ref[0]) + nshift_ref[0]).astype(h_ref.dtype)


def _outproj_ln(u2d, w_bf16, x2d, gate, ln_g, ln_b, next_mod=None, *, seq, tm, sub=256):
    m, kdim = u2d.shape
    d = w_bf16.shape[1]
    assert seq % tm == 0 and tm % sub == 0
    tiles_per_seq = seq // tm
    row = pl.BlockSpec((1, d), lambda i: (0, 0))
    mod_spec = pl.BlockSpec((1, 1, d), lambda i: (i // tiles_per_seq, 0, 0))
    tile = pl.BlockSpec((tm, d), lambda i: (i, 0))
    emit_h = next_mod is not None
    in_specs = [
        pl.BlockSpec((tm, kdim), lambda i: (i, 0)),
        pl.BlockSpec((kdim, d), lambda i: (0, 0), pipeline_mode=pl.Buffered(1)),
        tile, mod_spec, row, row,
    ]
    args = [u2d, w_bf16, x2d, gate, ln_g.reshape(1, d), ln_b.reshape(1, d)]
    out_specs, out_shape = tile, jax.ShapeDtypeStruct((m, d), F32)
    if emit_h:
        in_specs += [mod_spec, mod_spec]
        args += list(next_mod)
        out_specs = [tile, tile]
        out_shape = [out_shape, jax.ShapeDtypeStruct((m, d), BF16)]
    return pl.pallas_call(
        functools.partial(_outproj_ln_kernel, sub=sub, emit_h=emit_h),
        grid=(m // tm,),
        in_specs=in_specs,
        out_specs=out_specs,
        out_shape=out_shape,
        compiler_params=_params(("parallel",)),
        name="outproj_ln",
    )(*args)


def kernel(x, c, positions, ret_w_in, ret_w_out, moba_w_in, moba_w_out, w_ada, b_ada, ln_g, ln_b):
    b, s, d = x.shape
    assert DEPTH == 2 and w_ada.shape[0] == DEPTH
    ret_dk = d // RET_HEADS
    ret_dv = 2 * d // RET_HEADS
    moba_dh = d // MOBA_HEADS
    moba_half = moba_dh // 8

    mod = _ada(c, w_ada, b_ada)
    shift, scale, gate = [[mod[l, :, k * d:(k + 1) * d].reshape(b, 1, d) for l in range(DEPTH)]
                          for k in range(3)]
    cos_r, sin_r, mc, ms = _rope_tables(positions, ret_dk, 2 * moba_half)

    x2d = x.reshape(b * s, d)
    n_qk = 2 * RET_HEADS * ret_dk
    w_in = ret_w_in[0].astype(BF16)
    qk, h2d = _inproj_qk(x2d, shift[0], scale[0], w_in,
                         cos_r.reshape(b * s, LANES), sin_r.reshape(b * s, LANES),
                         n=n_qk, seq=s, head_w=ret_dk, k_mult=ret_dk ** -0.5)
    vg = _inproj_plain(h2d, w_in, col0=n_qk)
    u = _retention_core(qk.reshape(b, s, -1), vg.reshape(b, s, -1),
                        heads=RET_HEADS, dk=ret_dk, dv=ret_dv)
    x2d, h2d = _outproj_ln(u.reshape(b * s, -1), ret_w_out[0].astype(BF16), x2d, gate[0],
                           ln_g[0], ln_b[0], next_mod=(shift[1], scale[1]), seq=s, tm=256)

    proj = _inproj_plain(h2d, _moba_w_in_bf16(moba_w_in[0], 2 * MOBA_HEADS, moba_dh, moba_half))
    u = _moba_core(proj.reshape(b, s, -1), mc, ms, heads=MOBA_HEADS, dh=moba_dh)
    x2d = _outproj_ln(u.reshape(b * s, -1), moba_w_out[0].astype(BF16), x2d, gate[1],
                      ln_g[1], ln_b[1], seq=s, tm=512)
    return x2d.reshape(b, s, d)
```

```python
import functools
import math

import jax
import jax.numpy as jnp
import numpy as np
from jax import lax
from jax.experimental import pallas as pl
from jax.experimental.pallas import tpu as pltpu

F32 = jnp.float32
BF16 = jnp.bfloat16

DEPTH = 2
N_MIXERS = 2
DEEPNORM_ALPHA = (2.0 * DEPTH) ** 0.25
LN_EPS = 1e-5

RET_HEADS = 8
RET_CHUNK = 256
RET_THETA = 10000.0
RET_GN_EPS = 1e-5

MOBA_HEADS = 16
MOBA_BLOCK = 256
MOBA_TOPK = 3
MOBA_SCORE_ROWS = 256
ROPE_THETA = 500000.0
NEG = -1e30

LANES = 128
BF16_SUBLANES = 16
VMEM_LIMIT = 56 * 1024 * 1024


def _params(sem):
    return pltpu.CompilerParams(dimension_semantics=sem, vmem_limit_bytes=VMEM_LIMIT)


def _ada_kernel(c_ref, w_ref, b_ref, o_ref):
    acc = jnp.dot(c_ref[...].astype(BF16), w_ref[...].astype(BF16),
                  preferred_element_type=F32)
    o_ref[...] = acc + b_ref[...]


def _ada(c, w_ada, b_ada, tn=512):
    depth, d, n = w_ada.shape
    b = c.shape[0]
    return pl.pallas_call(
        _ada_kernel,
        grid=(depth, n // tn),
        in_specs=[
            pl.BlockSpec((b, d), lambda l, j: (0, 0)),
            pl.BlockSpec((None, d, tn), lambda l, j: (l, 0, j)),
            pl.BlockSpec((None, 1, tn), lambda l, j: (l, 0, j)),
        ],
        out_specs=pl.BlockSpec((None, b, tn), lambda l, j: (l, 0, j)),
        out_shape=jax.ShapeDtypeStruct((depth, b, n), F32),
        compiler_params=_params(("parallel", "parallel")),
        name="ada_mod",
    )(c, w_ada, b_ada.reshape(depth, 1, n))


def _rope_tables_kernel(pos_ref, inv_r_ref, inv_m_ref, cos_r, sin_r, mc, ms, *, half_m):
    pos = pos_ref[0].astype(F32)
    ang = pos * inv_r_ref[...]
    cos_r[0] = jnp.cos(ang)
    sin_r[0] = jnp.sin(ang)
    angm = pos * inv_m_ref[...]
    cm = jnp.cos(angm)
    sm = jnp.sin(angm)
    lane = lax.broadcasted_iota(jnp.int32, angm.shape, 1)
    first = lane < half_m
    second = (lane >= LANES // 2) & (lane < LANES // 2 + half_m)
    mc[0] = jnp.where(first | second, cm, 1.0)
    ms[0] = jnp.where(first, -sm, jnp.where(second, sm, 0.0))


def _rope_tables(positions, ret_dk, moba_rot, ts=512):
    b, s = positions.shape
    half_r = ret_dk // 2
    half_m = moba_rot // 2
    assert half_r == LANES and LANES % half_m == 0
    inv_r = RET_THETA ** (-jnp.arange(half_r, dtype=F32) * 2.0 / ret_dk)
    inv_m = ROPE_THETA ** (-jnp.arange(half_m, dtype=F32) * 2.0 / moba_rot)
    inv_m = jnp.tile(inv_m, LANES // half_m)
    tab = jax.ShapeDtypeStruct((b, s, LANES), F32)
    spec = pl.BlockSpec((1, ts, LANES), lambda i, j: (i, j, 0))
    row = pl.BlockSpec((1, LANES), lambda i, j: (0, 0))
    return pl.pallas_call(
        functools.partial(_rope_tables_kernel, half_m=half_m),
        grid=(b, s // ts),
        in_specs=[pl.BlockSpec((1, ts, 1), lambda i, j: (i, j, 0)), row, row],
        out_specs=[spec] * 4,
        out_shape=[tab] * 4,
        compiler_params=_params(("parallel", "parallel")),
        name="rope_tables",
    )(positions.reshape(b, s, 1), inv_r.reshape(1, LANES), inv_m.reshape(1, LANES))


def _moba_w_prep_kernel(w_ref, o_ref, *, half):
    mid = LANES // 2
    for hh in range(w_ref.shape[1] // LANES):
        cols = slice(hh * LANES, (hh + 1) * LANES)
        x = w_ref[:, cols]
        lane = lax.broadcasted_iota(jnp.int32, x.shape, 1)
        up = pltpu.roll(x, mid - half, 1)
        dn = pltpu.roll(x, LANES - (mid - half), 1)
        y = jnp.where((lane >= half) & (lane < 2 * half), dn,
                      jnp.where((lane >= mid) & (lane < mid + half), up, x))
        o_ref[:, cols] = y.astype(o_ref.dtype)


def _moba_w_qk_bf16(w, n_qk, dh, half, tn=1024):
    d = w.shape[0]
    assert dh == LANES and n_qk % tn == 0
    spec = pl.BlockSpec((d, tn), lambda j: (0, j))
    return pl.pallas_call(
        functools.partial(_moba_w_prep_kernel, half=half),
        grid=(n_qk // tn,),
        in_specs=[spec],
        out_specs=spec,
        out_shape=jax.ShapeDtypeStruct((d, n_qk), BF16),
        compiler_params=_params(("parallel",)),
        name="moba_w_prep",
    )(w)


def _inproj_qk_kernel(x_ref, shift_ref, scale_ref, w_ref, cos_ref, sin_ref, o_ref, h_ref,
                      *, tn, n_q_tiles, head_w, k_mult):
    j = pl.program_id(1)

    @pl.when(j == 0)
    def _():
        h = x_ref[...] * (1.0 + scale_ref[0]) + shift_ref[0]
        h_ref[...] = h.astype(h_ref.dtype)

    acc = jnp.dot(h_ref[...], w_ref[...].astype(h_ref.dtype),
                  preferred_element_type=F32)
    mult = jnp.where(j < n_q_tiles, 1.0, k_mult)
    c = cos_ref[...] * mult
    s = sin_ref[...] * mult
    for hh in range(tn // head_w):
        lo = hh * head_w
        x1 = acc[:, lo:lo + LANES]
        x2 = acc[:, lo + LANES:lo + 2 * LANES]
        o_ref[:, lo:lo + LANES] = (x1 * c - x2 * s).astype(o_ref.dtype)
        o_ref[:, lo + LANES:lo + 2 * LANES] = (x2 * c + x1 * s).astype(o_ref.dtype)


def _inproj_qk(x2d, shift, scale, w_mat, cos_t, sin_t, *, n, seq, head_w, k_mult,
               tm=1024, tn=1024):
    m, d = x2d.shape
    assert seq % tm == 0 and (n // 2) % tn == 0 and tn % head_w == 0 and head_w == 2 * LANES
    tiles_per_seq = seq // tm
    tab_spec = pl.BlockSpec((tm, LANES), lambda i, j: (i, 0))
    mod_spec = pl.BlockSpec((1, 1, d), lambda i, j: (i // tiles_per_seq, 0, 0))
    return pl.pallas_call(
        functools.partial(_inproj_qk_kernel, tn=tn, n_q_tiles=n // 2 // tn,
                          head_w=head_w, k_mult=k_mult),
        grid=(m // tm, n // tn),
        in_specs=[
            pl.BlockSpec((tm, d), lambda i, j: (i, 0)),
            mod_spec, mod_spec,
            pl.BlockSpec((d, tn), lambda i, j: (0, j)),
            tab_spec, tab_spec,
        ],
        out_specs=[pl.BlockSpec((tm, tn), lambda i, j: (i, j)),
                   pl.BlockSpec((tm, d), lambda i, j: (i, 0))],
        out_shape=[jax.ShapeDtypeStruct((m, n), BF16), jax.ShapeDtypeStruct((m, d), BF16)],
        compiler_params=_params(("parallel", "arbitrary")),
        name="inproj_qk",
    )(x2d, shift, scale, w_mat, cos_t, sin_t)


def _matmul_kernel(h_ref, w_ref, o_ref):
    o_ref[...] = jnp.dot(h_ref[...], w_ref[...].astype(h_ref.dtype),
                         preferred_element_type=F32).astype(o_ref.dtype)


def _inproj_plain(h2d, w_mat, *, col0=0, tm=2048, tn=1024):
    m, d = h2d.shape
    n = w_mat.shape[1] - col0
    assert m % tm == 0 and n % tn == 0 and col0 % tn == 0
    j0 = col0 // tn
    return pl.pallas_call(
        _matmul_kernel,
        grid=(m // tm, n // tn),
        in_specs=[pl.BlockSpec((tm, d), lambda i, j: (i, 0)),
                  pl.BlockSpec((d, tn), lambda i, j: (0, j0 + j))],
        out_specs=pl.BlockSpec((tm, tn), lambda i, j: (i, j)),
        out_shape=jax.ShapeDtypeStruct((m, n), BF16),
        compiler_params=_params(("parallel", "parallel")),
        name="inproj_plain",
    )(h2d, w_mat)


def _silu(g):
    half = 0.5 * g
    return half + half * jnp.tanh(half)


def _retention_kernel(q_ref, k_ref, v_ref, g_ref, dec_ref, qd_ref, kd_ref, cd_ref, o_ref, st_scrs,
                      *, n_chunks, chunk, dk, dv):
    n_heads = q_ref.shape[2] // dk
    programs = [_retention_head_program(hh, q_ref, k_ref, v_ref, g_ref, dec_ref.at[hh],
                                        qd_ref.at[hh], kd_ref.at[hh], cd_ref.at[hh], o_ref,
                                        st_scrs.at[hh], n_chunks=n_chunks, chunk=chunk,
                                        dk=dk, dv=dv)
                for hh in range(n_heads)]
    while programs:
        programs = [p for p in programs if next(p, "done") != "done"]


def _retention_head_program(hh, q_ref, k_ref, v_ref, g_ref, dec_ref, qd_ref, kd_ref, cd_ref,
                            o_ref, st_scr, *, n_chunks, chunk, dk, dv):
    qk_cols = slice(hh * dk, (hh + 1) * dk)
    vg_cols = slice(hh * dv, (hh + 1) * dv)
    st_scr[...] = jnp.zeros_like(st_scr)
    qd = qd_ref[...]
    cd = cd_ref[...]

    def state_free(ci):
        rows = slice(ci * chunk, (ci + 1) * chunk)
        qc = q_ref[0, rows, qk_cols]
        kc = k_ref[0, rows, qk_cols]
        vc = v_ref[0, rows, vg_cols]
        scores = lax.dot_general(qc, kc, (((1,), (1,)), ((), ())),
                                 preferred_element_type=F32) * dec_ref[...]
        inner = jnp.dot(scores.astype(BF16), vc, preferred_element_type=F32)
        kdec = (kc.astype(F32) * kd_ref[...]).astype(BF16)
        update = lax.dot_general(kdec, vc, (((0,), (0,)), ((), ())), preferred_element_type=F32)
        return inner, update

    def finish(ci, inner, update):
        rows = slice(ci * chunk, (ci + 1) * chunk)
        state = st_scr[...]
        cross = jnp.dot(q_ref[0, rows, qk_cols], state.astype(BF16),
                        preferred_element_type=F32) * qd
        st_scr[...] = state * cd + update
        o = inner + cross
        mu = jnp.mean(o, axis=-1, keepdims=True)
        oc = o - mu
        var = jnp.mean(oc * oc, axis=-1, keepdims=True)
        gn = oc * lax.rsqrt(var + RET_GN_EPS)
        g = g_ref[0, rows, vg_cols].astype(F32)
        o_ref[0, rows, vg_cols] = (gn * _silu(g)).astype(o_ref.dtype)

    nxt = state_free(0)
    yield
    for ci in range(n_chunks):
        cur = nxt
        if ci + 1 < n_chunks:
            nxt = state_free(ci + 1)
            yield
        finish(ci, *cur)
        yield


def _retention_decays(heads, chunk):
    log_gamma = jnp.log(1.0 - 2.0 ** (-5.0 - jnp.arange(heads, dtype=F32)))
    n = jnp.arange(chunk, dtype=F32)
    diff = n[:, None] - n[None, :]
    inner = jnp.where(diff >= 0, jnp.exp(jnp.maximum(diff, 0.0) * log_gamma[:, None, None]), 0.0)
    q_decay = jnp.exp((n + 1.0) * log_gamma[:, None])
    k_decay = jnp.exp((chunk - 1.0 - n) * log_gamma[:, None])
    chunk_decay = jnp.exp(chunk * log_gamma)
    return (inner.astype(F32), q_decay.reshape(heads, chunk, 1), k_decay.reshape(heads, chunk, 1),
            chunk_decay.reshape(heads, 1, 1))


def _retention_core(qk, vg, *, heads, dk, dv, heads_per_step=2):
    b, s, _ = qk.shape
    chunk = RET_CHUNK
    hp = heads_per_step
    assert s % chunk == 0 and heads % hp == 0
    groups = heads // hp
    dec, qd, kd, cd = _retention_decays(heads, chunk)
    return pl.pallas_call(
        functools.partial(_retention_kernel, n_chunks=s // chunk, chunk=chunk, dk=dk, dv=dv),
        grid=(b, groups),
        in_specs=[
            pl.BlockSpec((1, s, hp * dk), lambda i, h: (i, 0, h)),
            pl.BlockSpec((1, s, hp * dk), lambda i, h: (i, 0, groups + h)),
            pl.BlockSpec((1, s, hp * dv), lambda i, h: (i, 0, h)),
            pl.BlockSpec((1, s, hp * dv), lambda i, h: (i, 0, groups + h)),
            pl.BlockSpec((hp, chunk, chunk), lambda i, h: (h, 0, 0)),
            pl.BlockSpec((hp, chunk, 1), lambda i, h: (h, 0, 0)),
            pl.BlockSpec((hp, chunk, 1), lambda i, h: (h, 0, 0)),
            pl.BlockSpec((hp, 1, 1), lambda i, h: (h, 0, 0)),
        ],
        out_specs=pl.BlockSpec((1, s, hp * dv), lambda i, h: (i, 0, h)),
        out_shape=jax.ShapeDtypeStruct((b, s, heads * dv), BF16),
        scratch_shapes=[pltpu.VMEM((hp, dk, dv), F32)],
        compiler_params=_params(("parallel", "parallel")),
        name="retention_core",
    )(qk, qk, vg, vg, dec, qd, kd, cd)


def _moba_kernel(q_ref, k_ref, v_ref, g_ref, mc_ref, ms_ref, o_ref,
                 qr_scrs, kr_scrs, vt_scrs, km_scrs, *s_scrs,
                 n_blocks, blk, sub, topk, exp_scale, dh):
    n_heads = q_ref.shape[2] // dh
    programs = [_moba_head_program(hh, q_ref, k_ref, v_ref, g_ref, mc_ref, ms_ref, o_ref,
                                   qr_scrs.at[hh], kr_scrs.at[hh], vt_scrs.at[hh], km_scrs.at[hh],
                                   s_scrs[hh * n_blocks:(hh + 1) * n_blocks],
                                   n_blocks=n_blocks, blk=blk, sub=sub, topk=topk,
                                   exp_scale=exp_scale, dh=dh)
                for hh in range(n_heads)]
    while programs:
        programs = [p for p in programs if next(p, "done") != "done"]


def _moba_head_program(hh, q_ref, k_ref, v_ref, g_ref, mc_ref, ms_ref, o_ref,
                       qr_scr, kr_scr, vt_scr, km_scr, s_scrs, *,
                       n_blocks, blk, sub, topk, exp_scale, dh):
    lanes = slice(hh * dh, (hh + 1) * dh)
    vt_scr[dh:, :] = jnp.ones((vt_scr.shape[0] - dh, vt_scr.shape[1]), BF16)
    km_scr[...] = jnp.zeros_like(km_scr)

    def prepare(b):
        rows = slice(b * blk, (b + 1) * blk)
        mc = mc_ref[0, rows, :]
        ms = ms_ref[0, rows, :]

        def rotary(x_ref):
            x = x_ref[0, rows, lanes].astype(F32)
            return x * mc + pltpu.roll(x, LANES // 2, 1) * ms

        qr_scr[rows, :] = (rotary(q_ref) * exp_scale).astype(BF16)
        kr = rotary(k_ref)
        kr_scr[rows, :] = kr.astype(BF16)
        km_scr[b:b + 1, :] = jnp.mean(kr, axis=0, keepdims=True)
        vt_scr[:dh, rows] = v_ref[0, rows, lanes].T

    nt = (((1,), (1,)), ((), ()))
    key_id = lax.broadcasted_iota(jnp.int32, (sub, blk), 0)
    qry_id = lax.broadcasted_iota(jnp.int32, (sub, blk), 1)
    blk_id = lax.broadcasted_iota(jnp.int32, (n_blocks, blk), 0)

    def scores(i):
        cols = slice(i * blk, (i + 1) * blk)
        s_scr = s_scrs[i]
        qi = qr_scr[cols, :]
        ranked = i > topk
        if ranked:
            gi = lax.dot_general(km_scr[...].astype(BF16), qi, nt,
                                 preferred_element_type=F32)
            past = blk_id < i
        m = None
        bias = []
        for j in range(i + 1):
            mj = None
            for part in range(blk // sub):
                keys = slice(j * blk + part * sub, j * blk + (part + 1) * sub)
                sj = lax.dot_general(kr_scr[keys, :], qi, nt,
                                     preferred_element_type=F32)
                if j == i:
                    sj = jnp.where(key_id + part * sub <= qry_id, sj, NEG)
                s_scr[keys, :] = sj
                mp = jnp.max(sj, axis=0, keepdims=True)
                mj = mp if mj is None else jnp.maximum(mj, mp)
            if ranked and j < i:
                gj = gi[j:j + 1, :]
                beats = past & ((gi > gj) | ((gi == gj) & (blk_id < j)))
                cnt = jnp.sum(jnp.where(beats, 1.0, 0.0), axis=0, keepdims=True)
                bias.append(jnp.where(cnt < topk, 0.0, NEG))
                mj = mj + bias[-1]
            m = mj if m is None else jnp.maximum(m, mj)
        if not ranked:
            return [m]
        return [m - bj for bj in bias] + [m]

    def attend(i, subs):
        cols = slice(i * blk, (i + 1) * blk)
        ot = None
        for j in range(i + 1):
            keys = slice(j * blk, (j + 1) * blk)
            e = jnp.exp2(s_scrs[i][keys, :] - subs[min(j, len(subs) - 1)]).astype(BF16)
            pj = jnp.dot(vt_scr[:, keys], e, preferred_element_type=F32)
            ot = pj if ot is None else pj + ot
        o = (ot[:dh, :] * (1.0 / ot[dh:dh + 1, :])).T
        g = g_ref[0, cols, lanes].astype(F32)
        o_ref[0, cols, lanes] = (o * _silu(g)).astype(o_ref.dtype)

    order = list(range(1, n_blocks, 2)) + list(range((n_blocks - 1) // 2 * 2, -1, -2))
    prepared = 0

    def prepare_upto(b):
        nonlocal prepared
        for blk_i in range(prepared, b + 1):
            prepare(blk_i)
        prepared = max(prepared, b + 1)

    prepare_upto(order[0])
    yield
    nxt = scores(order[0])
    yield
    for pos, i in enumerate(order):
        cur = nxt
        if pos + 2 < n_blocks:
            prepare_upto(order[pos + 2])
        if pos + 1 < n_blocks:
            prepare_upto(order[pos + 1])
            nxt = scores(order[pos + 1])
            yield
        attend(i, cur)
        yield


def _moba_core(qk, vg, mc, ms, *, heads, dh, heads_per_step=2):
    b, s, _ = qk.shape
    blk = MOBA_BLOCK
    hp = heads_per_step
    assert s % blk == 0 and dh == LANES and heads % hp == 0
    n_blocks = s // blk
    groups = heads // hp
    spec = lambda off: pl.BlockSpec((1, s, hp * dh), lambda i, h: (i, 0, off + h))
    tab = pl.BlockSpec((1, s, dh), lambda i, h: (i, 0, 0))
    return pl.pallas_call(
        functools.partial(_moba_kernel, n_blocks=n_blocks, blk=blk, sub=MOBA_SCORE_ROWS,
                          topk=min(MOBA_TOPK, n_blocks),
                          exp_scale=dh ** -0.5 * math.log2(math.e), dh=dh),
        grid=(b, groups),
        in_specs=[spec(0), spec(groups), spec(0), spec(groups), tab, tab],
        out_specs=spec(0),
        out_shape=jax.ShapeDtypeStruct((b, s, heads * dh), BF16),
        scratch_shapes=[pltpu.VMEM((hp, s, dh), BF16), pltpu.VMEM((hp, s, dh), BF16),
                        pltpu.VMEM((hp, dh + BF16_SUBLANES, s), BF16),
                        pltpu.VMEM((hp, n_blocks, dh), F32)]
        + [pltpu.VMEM(((i + 1) * blk, blk), F32) for _ in range(hp) for i in range(n_blocks)],
        compiler_params=_params(("parallel", "parallel")),
        name="moba_core",
    )(qk, qk, vg, vg, mc, ms)


def _outproj_ln_kernel(u_ref, w_ref, x_ref, gate_ref, lng_ref, lnb_ref, *rest, sub, emit_h):
    if emit_h:
        nshift_ref, nscale_ref, o_ref, h_ref = rest
    else:
        (o_ref,) = rest
    n_sub = u_ref.shape[0] // sub

    def matmul(r):
        return jnp.dot(u_ref[r * sub:(r + 1) * sub, :], w_ref[...].astype(u_ref.dtype),
                       preferred_element_type=F32)

    y_next = matmul(0)
    for r in range(n_sub):
        rows = slice(r * sub, (r + 1) * sub)
        y = y_next
        if r + 1 < n_sub:
            y_next = matmul(r + 1)
        o_ref[rows, :] = DEEPNORM_ALPHA * x_ref[rows, :] + gate_ref[0] * y
        mu = jnp.mean(o_ref[rows, :], axis=-1, keepdims=True)
        var = jnp.mean(jnp.square(o_ref[rows, :] - mu), axis=-1, keepdims=True)
        xn = (o_ref[rows, :] - mu) * lax.rsqrt(var + LN_EPS) * lng_ref[...] + lnb_ref[...]
        o_ref[rows, :] = xn
        if emit_h:
            h_ref[rows, :] = (xn * (1.0 + nscale_ref[0]) + nshift_ref[0]).astype(h_ref.dtype)


def _outproj_ln(u2d, w_mat, x2d, gate, ln_g, ln_b, next_mod=None, *, seq, tm, sub=256):
    m, kdim = u2d.shape
    d = w_mat.shape[1]
    assert seq % tm == 0 and tm % sub == 0
    tiles_per_seq = seq // tm
    row = pl.BlockSpec((1, d), lambda i: (0, 0))
    mod_spec = pl.BlockSpec((1, 1, d), lambda i: (i // tiles_per_seq, 0, 0))
    tile = pl.BlockSpec((tm, d), lambda i: (i, 0))
    emit_h = next_mod is not None
    in_specs = [
        pl.BlockSpec((tm, kdim), lambda i: (i, 0)),
        pl.BlockSpec((kdim, d), lambda i: (0, 0), pipeline_mode=pl.Buffered(1)),
        tile, mod_spec, row, row,
    ]
    args = [u2d, w_mat, x2d, gate, ln_g.reshape(1, d), ln_b.reshape(1, d)]
    out_specs, out_shape = tile, jax.ShapeDtypeStruct((m, d), F32)
    if emit_h:
        in_specs += [mod_spec, mod_spec]
        args += list(next_mod)
        out_specs = [tile, tile]
        out_shape = [out_shape, jax.ShapeDtypeStruct((m, d), BF16)]
    return pl.pallas_call(
        functools.partial(_outproj_ln_kernel, sub=sub, emit_h=emit_h),
        grid=(m // tm,),
        in_specs=in_specs,
        out_specs=out_specs,
        out_shape=out_shape,
        compiler_params=_params(("parallel",)),
        name="outproj_ln",
    )(*args)


def kernel(x, c, positions, ret_w_in, ret_w_out, moba_w_in, moba_w_out, w_ada, b_ada, ln_g, ln_b):
    b, s, d = x.shape
    assert DEPTH == 2 and w_ada.shape[0] == DEPTH
    ret_dk = d // RET_HEADS
    ret_dv = 2 * d // RET_HEADS
    moba_dh = d // MOBA_HEADS
    moba_half = moba_dh // 8

    mod = _ada(c, w_ada, b_ada)
    shift, scale, gate = [[mod[l, :, k * d:(k + 1) * d].reshape(b, 1, d) for l in range(DEPTH)]
                          for k in range(3)]
    cos_r, sin_r, mc, ms = _rope_tables(positions, ret_dk, 2 * moba_half)

    x2d = x.reshape(b * s, d)
    n_qk = 2 * RET_HEADS * ret_dk
    qk, h2d = _inproj_qk(x2d, shift[0], scale[0], ret_w_in[0],
                         cos_r.reshape(b * s, LANES), sin_r.reshape(b * s, LANES),
                         n=n_qk, seq=s, head_w=ret_dk, k_mult=ret_dk ** -0.5)
    vg = _inproj_plain(h2d, ret_w_in[0], col0=n_qk)
    u = _retention_core(qk.reshape(b, s, -1), vg.reshape(b, s, -1),
                        heads=RET_HEADS, dk=ret_dk, dv=ret_dv)
    x2d, h2d = _outproj_ln(u.reshape(b * s, -1), ret_w_out[0].astype(BF16), x2d, gate[0],
                           ln_g[0], ln_b[0], next_mod=(shift[1], scale[1]), seq=s, tm=256)

    n_qk = 2 * MOBA_HEADS * moba_dh
    qk = _inproj_plain(h2d, _moba_w_qk_bf16(moba_w_in[0], n_qk, moba_dh, moba_half))
    vg = _inproj_plain(h2d, moba_w_in[0], col0=n_qk)
    u = _moba_core(qk.reshape(b, s, -1), vg.reshape(b, s, -1), mc, ms,
                   heads=MOBA_HEADS, dh=moba_dh)
    x2d = _outproj_ln(u.reshape(b * s, -1), moba_w_out[0], x2d, gate[1],
                      ln_g[1], ln_b[1], seq=s, tm=512)
    return x2d.reshape(b, s, d)
```

```python
import functools
import math

import jax
import jax.numpy as jnp
import numpy as np
from jax import lax
from jax.experimental import pallas as pl
from jax.experimental.pallas import tpu as pltpu

F32 = jnp.float32
BF16 = jnp.bfloat16

DEPTH = 2
N_MIXERS = 2
DEEPNORM_ALPHA = (2.0 * DEPTH) ** 0.25
LN_EPS = 1e-5

RET_HEADS = 8
RET_CHUNK = 256
RET_THETA = 10000.0
RET_GN_EPS = 1e-5

MOBA_HEADS = 16
MOBA_BLOCK = 256
MOBA_TOPK = 3
MOBA_SCORE_ROWS = 256
ROPE_THETA = 500000.0
NEG = -1e30

LANES = 128
BF16_SUBLANES = 16
VMEM_LIMIT = 56 * 1024 * 1024


def _params(sem):
    return pltpu.CompilerParams(dimension_semantics=sem, vmem_limit_bytes=VMEM_LIMIT)


def _ada_kernel(c_ref, w_ref, b_ref, o_ref):
    acc = jnp.dot(c_ref[...].astype(BF16), w_ref[...].astype(BF16),
                  preferred_element_type=F32)
    o_ref[...] = acc + b_ref[...]


def _ada(c, w_ada, b_ada, tn=512):
    depth, d, n = w_ada.shape
    b = c.shape[0]
    return pl.pallas_call(
        _ada_kernel,
        grid=(depth, n // tn),
        in_specs=[
            pl.BlockSpec((b, d), lambda l, j: (0, 0)),
            pl.BlockSpec((None, d, tn), lambda l, j: (l, 0, j)),
            pl.BlockSpec((None, 1, tn), lambda l, j: (l, 0, j)),
        ],
        out_specs=pl.BlockSpec((None, b, tn), lambda l, j: (l, 0, j)),
        out_shape=jax.ShapeDtypeStruct((depth, b, n), F32),
        compiler_params=_params(("parallel", "parallel")),
        name="ada_mod",
    )(c, w_ada, b_ada.reshape(depth, 1, n))


def _rope_tables_kernel(pos_ref, inv_r_ref, inv_m_ref, cos_r, sin_r, mc, ms, *, half_m):
    pos = pos_ref[0].astype(F32)
    ang = pos * inv_r_ref[...]
    cos_r[0] = jnp.cos(ang)
    sin_r[0] = jnp.sin(ang)
    angm = pos * inv_m_ref[...]
    cm = jnp.cos(angm)
    sm = jnp.sin(angm)
    lane = lax.broadcasted_iota(jnp.int32, angm.shape, 1)
    first = lane < half_m
    second = (lane >= LANES // 2) & (lane < LANES // 2 + half_m)
    mc[0] = jnp.where(first | second, cm, 1.0)
    ms[0] = jnp.where(first, -sm, jnp.where(second, sm, 0.0))


def _rope_tables(positions, ret_dk, moba_rot, ts=512):
    b, s = positions.shape
    half_r = ret_dk // 2
    half_m = moba_rot // 2
    assert half_r == LANES and LANES % half_m == 0
    inv_r = RET_THETA ** (-jnp.arange(half_r, dtype=F32) * 2.0 / ret_dk)
    inv_m = ROPE_THETA ** (-jnp.arange(half_m, dtype=F32) * 2.0 / moba_rot)
    inv_m = jnp.tile(inv_m, LANES // half_m)
    tab = jax.ShapeDtypeStruct((b, s, LANES), F32)
    spec = pl.BlockSpec((1, ts, LANES), lambda i, j: (i, j, 0))
    row = pl.BlockSpec((1, LANES), lambda i, j: (0, 0))
    return pl.pallas_call(
        functools.partial(_rope_tables_kernel, half_m=half_m),
        grid=(b, s // ts),
        in_specs=[pl.BlockSpec((1, ts, 1), lambda i, j: (i, j, 0)), row, row],
        out_specs=[spec] * 4,
        out_shape=[tab] * 4,
        compiler_params=_params(("parallel", "parallel")),
        name="rope_tables",
    )(positions.reshape(b, s, 1), inv_r.reshape(1, LANES), inv_m.reshape(1, LANES))


def _moba_w_prep_kernel(w_ref, o_ref, *, half):
    mid = LANES // 2
    for hh in range(w_ref.shape[1] // LANES):
        cols = slice(hh * LANES, (hh + 1) * LANES)
        x = w_ref[:, cols]
        lane = lax.broadcasted_iota(jnp.int32, x.shape, 1)
        up = pltpu.roll(x, mid - half, 1)
        dn = pltpu.roll(x, LANES - (mid - half), 1)
        y = jnp.where((lane >= half) & (lane < 2 * half), dn,
                      jnp.where((lane >= mid) & (lane < mid + half), up, x))
        o_ref[:, cols] = y.astype(o_ref.dtype)


def _moba_w_qk_bf16(w, n_qk, dh, half, tn=1024):
    d = w.shape[0]
    assert dh == LANES and n_qk % tn == 0
    spec = pl.BlockSpec((d, tn), lambda j: (0, j))
    return pl.pallas_call(
        functools.partial(_moba_w_prep_kernel, half=half),
        grid=(n_qk // tn,),
        in_specs=[spec],
        out_specs=spec,
        out_shape=jax.ShapeDtypeStruct((d, n_qk), BF16),
        compiler_params=_params(("parallel",)),
        name="moba_w_prep",
    )(w)


def _inproj_qk_kernel(x_ref, shift_ref, scale_ref, w_ref, cos_ref, sin_ref, o_ref, h_ref,
                      *, tn, head_w, k_mult):
    h = x_ref[...] * (1.0 + scale_ref[0]) + shift_ref[0]
    h_ref[...] = h.astype(h_ref.dtype)
    n = w_ref.shape[1]
    cos_t, sin_t = cos_ref[...], sin_ref[...]
    for c0 in range(0, n, tn):
        acc = jnp.dot(h_ref[...], w_ref[:, c0:c0 + tn], preferred_element_type=F32)
        is_q = c0 < n // 2
        c = cos_t if is_q else cos_t * k_mult
        s = sin_t if is_q else sin_t * k_mult
        for lo in range(0, tn, head_w):
            x1 = acc[:, lo:lo + LANES]
            x2 = acc[:, lo + LANES:lo + 2 * LANES]
            o_ref[:, c0 + lo:c0 + lo + LANES] = (x1 * c - x2 * s).astype(o_ref.dtype)
            o_ref[:, c0 + lo + LANES:c0 + lo + 2 * LANES] = (x2 * c + x1 * s).astype(o_ref.dtype)


def _inproj_qk(x2d, shift, scale, w_qk, cos_t, sin_t, *, seq, head_w, k_mult, tm=512, tn=1024):
    m, d = x2d.shape
    n = w_qk.shape[1]
    assert seq % tm == 0 and (n // 2) % tn == 0 and tn % head_w == 0 and head_w == 2 * LANES
    tiles_per_seq = seq // tm
    tab_spec = pl.BlockSpec((tm, LANES), lambda i: (i, 0))
    mod_spec = pl.BlockSpec((1, 1, d), lambda i: (i // tiles_per_seq, 0, 0))
    return pl.pallas_call(
        functools.partial(_inproj_qk_kernel, tn=tn, head_w=head_w, k_mult=k_mult),
        grid=(m // tm,),
        in_specs=[
            pl.BlockSpec((tm, d), lambda i: (i, 0)),
            mod_spec, mod_spec,
            pl.BlockSpec((d, n), lambda i: (0, 0), pipeline_mode=pl.Buffered(1)),
            tab_spec, tab_spec,
        ],
        out_specs=[pl.BlockSpec((tm, n), lambda i: (i, 0)),
                   pl.BlockSpec((tm, d), lambda i: (i, 0))],
        out_shape=[jax.ShapeDtypeStruct((m, n), BF16), jax.ShapeDtypeStruct((m, d), BF16)],
        compiler_params=_params(("parallel",)),
        name="inproj_qk",
    )(x2d, shift, scale, w_qk, cos_t, sin_t)


def _matmul_kernel(h_ref, w_ref, o_ref):
    o_ref[...] = jnp.dot(h_ref[...], w_ref[...].astype(h_ref.dtype),
                         preferred_element_type=F32).astype(o_ref.dtype)


def _inproj_plain(h2d, w_mat, *, col0=0, tm=2048, tn=1024):
    m, d = h2d.shape
    n = w_mat.shape[1] - col0
    assert m % tm == 0 and n % tn == 0 and col0 % tn == 0
    j0 = col0 // tn
    return pl.pallas_call(
        _matmul_kernel,
        grid=(m // tm, n // tn),
        in_specs=[pl.BlockSpec((tm, d), lambda i, j: (i, 0)),
                  pl.BlockSpec((d, tn), lambda i, j: (0, j0 + j))],
        out_specs=pl.BlockSpec((tm, tn), lambda i, j: (i, j)),
        out_shape=jax.ShapeDtypeStruct((m, n), BF16),
        compiler_params=_params(("parallel", "parallel")),
        name="inproj_plain",
    )(h2d, w_mat)


def _silu(g):
    half = 0.5 * g
    return half + half * jnp.tanh(half)


def _retention_kernel(q_ref, k_ref, v_ref, g_ref, dec_ref, qd_ref, kd_ref, cd_ref, o_ref, st_scrs,
                      *, n_chunks, chunk, dk, dv):
    n_heads = q_ref.shape[2] // dk
    programs = [_retention_head_program(hh, q_ref, k_ref, v_ref, g_ref, dec_ref.at[hh],
                                        qd_ref.at[hh], kd_ref.at[hh], cd_ref.at[hh], o_ref,
                                        st_scrs.at[hh], n_chunks=n_chunks, chunk=chunk,
                                        dk=dk, dv=dv)
                for hh in range(n_heads)]
    while programs:
        programs = [p for p in programs if next(p, "done") != "done"]


def _retention_head_program(hh, q_ref, k_ref, v_ref, g_ref, dec_ref, qd_ref, kd_ref, cd_ref,
                            o_ref, st_scr, *, n_chunks, chunk, dk, dv):
    qk_cols = slice(hh * dk, (hh + 1) * dk)
    vg_cols = slice(hh * dv, (hh + 1) * dv)
    st_scr[...] = jnp.zeros_like(st_scr)
    qd = qd_ref[...]
    cd = cd_ref[...]

    def state_free(ci):
        rows = slice(ci * chunk, (ci + 1) * chunk)
        qc = q_ref[0, rows, qk_cols]
        kc = k_ref[0, rows, qk_cols]
        vc = v_ref[0, rows, vg_cols]
        scores = lax.dot_general(qc, kc, (((1,), (1,)), ((), ())),
                                 preferred_element_type=F32) * dec_ref[...]
        inner = jnp.dot(scores.astype(BF16), vc, preferred_element_type=F32)
        kdec = (kc.astype(F32) * kd_ref[...]).astype(BF16)
        update = lax.dot_general(kdec, vc, (((0,), (0,)), ((), ())), preferred_element_type=F32)
        return inner, update

    def finish(ci, inner, update):
        rows = slice(ci * chunk, (ci + 1) * chunk)
        state = st_scr[...]
        cross = jnp.dot(q_ref[0, rows, qk_cols], state.astype(BF16),
                        preferred_element_type=F32) * qd
        st_scr[...] = state * cd + update
        o = inner + cross
        mu = jnp.mean(o, axis=-1, keepdims=True)
        oc = o - mu
        var = jnp.mean(oc * oc, axis=-1, keepdims=True)
        gn = oc * lax.rsqrt(var + RET_GN_EPS)
        g = g_ref[0, rows, vg_cols].astype(F32)
        o_ref[0, rows, vg_cols] = (gn * _silu(g)).astype(o_ref.dtype)

    nxt = state_free(0)
    yield
    for ci in range(n_chunks):
        cur = nxt
        if ci + 1 < n_chunks:
            nxt = state_free(ci + 1)
            yield
        finish(ci, *cur)
        yield


def _retention_decays(heads, chunk):
    log_gamma = jnp.log(1.0 - 2.0 ** (-5.0 - jnp.arange(heads, dtype=F32)))
    n = jnp.arange(chunk, dtype=F32)
    diff = n[:, None] - n[None, :]
    inner = jnp.where(diff >= 0, jnp.exp(jnp.maximum(diff, 0.0) * log_gamma[:, None, None]), 0.0)
    q_decay = jnp.exp((n + 1.0) * log_gamma[:, None])
    k_decay = jnp.exp((chunk - 1.0 - n) * log_gamma[:, None])
    chunk_decay = jnp.exp(chunk * log_gamma)
    return (inner.astype(F32), q_decay.reshape(heads, chunk, 1), k_decay.reshape(heads, chunk, 1),
            chunk_decay.reshape(heads, 1, 1))


def _retention_core(qk, vg, *, heads, dk, dv, heads_per_step=2):
    b, s, _ = qk.shape
    chunk = RET_CHUNK
    hp = heads_per_step
    assert s % chunk == 0 and heads % hp == 0
    groups = heads // hp
    dec, qd, kd, cd = _retention_decays(heads, chunk)
    return pl.pallas_call(
        functools.partial(_retention_kernel, n_chunks=s // chunk, chunk=chunk, dk=dk, dv=dv),
        grid=(b, groups),
        in_specs=[
            pl.BlockSpec((1, s, hp * dk), lambda i, h: (i, 0, h)),
            pl.BlockSpec((1, s, hp * dk), lambda i, h: (i, 0, groups + h)),
            pl.BlockSpec((1, s, hp * dv), lambda i, h: (i, 0, h)),
            pl.BlockSpec((1, s, hp * dv), lambda i, h: (i, 0, groups + h)),
            pl.BlockSpec((hp, chunk, chunk), lambda i, h: (h, 0, 0)),
            pl.BlockSpec((hp, chunk, 1), lambda i, h: (h, 0, 0)),
            pl.BlockSpec((hp, chunk, 1), lambda i, h: (h, 0, 0)),
            pl.BlockSpec((hp, 1, 1), lambda i, h: (h, 0, 0)),
        ],
        out_specs=pl.BlockSpec((1, s, hp * dv), lambda i, h: (i, 0, h)),
        out_shape=jax.ShapeDtypeStruct((b, s, heads * dv), BF16),
        scratch_shapes=[pltpu.VMEM((hp, dk, dv), F32)],
        compiler_params=_params(("parallel", "parallel")),
        name="retention_core",
    )(qk, qk, vg, vg, dec, qd, kd, cd)


def _moba_kernel(q_ref, k_ref, v_ref, g_ref, mc_ref, ms_ref, o_ref,
                 qr_scrs, kr_scrs, vt_scrs, km_scrs, *s_scrs,
                 n_blocks, blk, sub, topk, exp_scale, dh):
    n_heads = q_ref.shape[2] // dh
    programs = [_moba_head_program(hh, q_ref, k_ref, v_ref, g_ref, mc_ref, ms_ref, o_ref,
                                   qr_scrs.at[hh], kr_scrs.at[hh], vt_scrs.at[hh], km_scrs.at[hh],
                                   s_scrs[hh * n_blocks:(hh + 1) * n_blocks],
                                   n_blocks=n_blocks, blk=blk, sub=sub, topk=topk,
                                   exp_scale=exp_scale, dh=dh)
                for hh in range(n_heads)]
    while programs:
        programs = [p for p in programs if next(p, "done") != "done"]


def _moba_head_program(hh, q_ref, k_ref, v_ref, g_ref, mc_ref, ms_ref, o_ref,
                       qr_scr, kr_scr, vt_scr, km_scr, s_scrs, *,
                       n_blocks, blk, sub, topk, exp_scale, dh):
    lanes = slice(hh * dh, (hh + 1) * dh)
    vt_scr[dh:, :] = jnp.ones((vt_scr.shape[0] - dh, vt_scr.shape[1]), BF16)
    km_scr[...] = jnp.zeros_like(km_scr)

    def prepare(b):
        rows = slice(b * blk, (b + 1) * blk)
        mc = mc_ref[0, rows, :]
        ms = ms_ref[0, rows, :]

        def rotary(x_ref):
            x = x_ref[0, rows, lanes].astype(F32)
            return x * mc + pltpu.roll(x, LANES // 2, 1) * ms

        qr_scr[rows, :] = (rotary(q_ref) * exp_scale).astype(BF16)
        kr = rotary(k_ref)
        kr_scr[rows, :] = kr.astype(BF16)
        km_scr[b:b + 1, :] = jnp.mean(kr, axis=0, keepdims=True)
        vt_scr[:dh, rows] = v_ref[0, rows, lanes].T

    nt = (((1,), (1,)), ((), ()))
    key_id = lax.broadcasted_iota(jnp.int32, (sub, blk), 0)
    qry_id = lax.broadcasted_iota(jnp.int32, (sub, blk), 1)
    blk_id = lax.broadcasted_iota(jnp.int32, (n_blocks, blk), 0)

    def scores(i):
        cols = slice(i * blk, (i + 1) * blk)
        s_scr = s_scrs[i]
        qi = qr_scr[cols, :]
        ranked = i > topk
        if ranked:
            gi = lax.dot_general(km_scr[...].astype(BF16), qi, nt,
                                 preferred_element_type=F32)
            past = blk_id < i
        m = None
        bias = []
        for j in range(i + 1):
            mj = None
            for part in range(blk // sub):
                keys = slice(j * blk + part * sub, j * blk + (part + 1) * sub)
                sj = lax.dot_general(kr_scr[keys, :], qi, nt,
                                     preferred_element_type=F32)
                if j == i:
                    sj = jnp.where(key_id + part * sub <= qry_id, sj, NEG)
                s_scr[keys, :] = sj
                mp = jnp.max(sj, axis=0, keepdims=True)
                mj = mp if mj is None else jnp.maximum(mj, mp)
            if ranked and j < i:
                gj = gi[j:j + 1, :]
                beats = past & ((gi > gj) | ((gi == gj) & (blk_id < j)))
                cnt = jnp.sum(jnp.where(beats, 1.0, 0.0), axis=0, keepdims=True)
                bias.append(jnp.where(cnt < topk, 0.0, NEG))
                mj = mj + bias[-1]
            m = mj if m is None else jnp.maximum(m, mj)
        if not ranked:
            return [m]
        return [m - bj for bj in bias] + [m]

    def attend(i, subs):
        cols = slice(i * blk, (i + 1) * blk)
        ot = None
        for j in range(i + 1):
            keys = slice(j * blk, (j + 1) * blk)
            e = jnp.exp2(s_scrs[i][keys, :] - subs[min(j, len(subs) - 1)]).astype(BF16)
            pj = jnp.dot(vt_scr[:, keys], e, preferred_element_type=F32)
            ot = pj if ot is None else pj + ot
        o = (ot[:dh, :] * (1.0 / ot[dh:dh + 1, :])).T
        g = g_ref[0, cols, lanes].astype(F32)
        o_ref[0, cols, lanes] = (o * _silu(g)).astype(o_ref.dtype)

    order = list(range(1, n_blocks, 2)) + list(range((n_blocks - 1) // 2 * 2, -1, -2))
    prepared = 0

    def prepare_upto(b):
        nonlocal prepared
        for blk_i in range(prepared, b + 1):
            prepare(blk_i)
        prepared = max(prepared, b + 1)

    prepare_upto(order[0])
    yield
    nxt = scores(order[0])
    yield
    for pos, i in enumerate(order):
        cur = nxt
        if pos + 2 < n_blocks:
            prepare_upto(order[pos + 2])
        if pos + 1 < n_blocks:
            prepare_upto(order[pos + 1])
            nxt = scores(order[pos + 1])
            yield
        attend(i, cur)
        yield


def _moba_core(qk, vg, mc, ms, *, heads, dh, heads_per_step=2):
    b, s, _ = qk.shape
    blk = MOBA_BLOCK
    hp = heads_per_step
    assert s % blk == 0 and dh == LANES and heads % hp == 0
    n_blocks = s // blk
    groups = heads // hp
    spec = lambda off: pl.BlockSpec((1, s, hp * dh), lambda i, h: (i, 0, off + h))
    tab = pl.BlockSpec((1, s, dh), lambda i, h: (i, 0, 0))
    return pl.pallas_call(
        functools.partial(_moba_kernel, n_blocks=n_blocks, blk=blk, sub=MOBA_SCORE_ROWS,
                          topk=min(MOBA_TOPK, n_blocks),
                          exp_scale=dh ** -0.5 * math.log2(math.e), dh=dh),
        grid=(b, groups),
        in_specs=[spec(0), spec(groups), spec(0), spec(groups), tab, tab],
        out_specs=spec(0),
        out_shape=jax.ShapeDtypeStruct((b, s, heads * dh), BF16),
        scratch_shapes=[pltpu.VMEM((hp, s, dh), BF16), pltpu.VMEM((hp, s, dh), BF16),
                        pltpu.VMEM((hp, dh + BF16_SUBLANES, s), BF16),
                        pltpu.VMEM((hp, n_blocks, dh), F32)]
        + [pltpu.VMEM(((i + 1) * blk, blk), F32) for _ in range(hp) for i in range(n_blocks)],
        compiler_params=_params(("parallel", "parallel")),
        name="moba_core",
    )(qk, qk, vg, vg, mc, ms)


def _outproj_ln_kernel(u_ref, w_ref, x_ref, gate_ref, lng_ref, lnb_ref, *rest, sub, emit_h):
    if emit_h:
        nshift_ref, nscale_ref, o_ref, h_ref = rest
    else:
        (o_ref,) = rest
    n_sub = u_ref.shape[0] // sub

    def matmul(r):
        return jnp.dot(u_ref[r * sub:(r + 1) * sub, :], w_ref[...].astype(u_ref.dtype),
                       preferred_element_type=F32)

    y_next = matmul(0)
    for r in range(n_sub):
        rows = slice(r * sub, (r + 1) * sub)
        y = y_next
        if r + 1 < n_sub:
            y_next = matmul(r + 1)
        o_ref[rows, :] = DEEPNORM_ALPHA * x_ref[rows, :] + gate_ref[0] * y
        mu = jnp.mean(o_ref[rows, :], axis=-1, keepdims=True)
        var = jnp.mean(jnp.square(o_ref[rows, :] - mu), axis=-1, keepdims=True)
        xn = (o_ref[rows, :] - mu) * lax.rsqrt(var + LN_EPS) * lng_ref[...] + lnb_ref[...]
        o_ref[rows, :] = xn
        if emit_h:
            h_ref[rows, :] = (xn * (1.0 + nscale_ref[0]) + nshift_ref[0]).astype(h_ref.dtype)


def _outproj_ln(u2d, w_mat, x2d, gate, ln_g, ln_b, next_mod=None, *, seq, tm, sub=256):
    m, kdim = u2d.shape
    d = w_mat.shape[1]
    assert seq % tm == 0 and tm % sub == 0
    tiles_per_seq = seq // tm
    row = pl.BlockSpec((1, d), lambda i: (0, 0))
    mod_spec = pl.BlockSpec((1, 1, d), lambda i: (i // tiles_per_seq, 0, 0))
    tile = pl.BlockSpec((tm, d), lambda i: (i, 0))
    emit_h = next_mod is not None
    in_specs = [
        pl.BlockSpec((tm, kdim), lambda i: (i, 0)),
        pl.BlockSpec((kdim, d), lambda i: (0, 0), pipeline_mode=pl.Buffered(1)),
        tile, mod_spec, row, row,
    ]
    args = [u2d, w_mat, x2d, gate, ln_g.reshape(1, d), ln_b.reshape(1, d)]
    out_specs, out_shape = tile, jax.ShapeDtypeStruct((m, d), F32)
    if emit_h:
        in_specs += [mod_spec, mod_spec]
        args += list(next_mod)
        out_specs = [tile, tile]
        out_shape = [out_shape, jax.ShapeDtypeStruct((m, d), BF16)]
    return pl.pallas_call(
        functools.partial(_outproj_ln_kernel, sub=sub, emit_h=emit_h),
        grid=(m // tm,),
        in_specs=in_specs,
        out_specs=out_specs,
        out_shape=out_shape,
        compiler_params=_params(("parallel",)),
        name="outproj_ln",
    )(*args)


def kernel(x, c, positions, ret_w_in, ret_w_out, moba_w_in, moba_w_out, w_ada, b_ada, ln_g, ln_b):
    b, s, d = x.shape
    assert DEPTH == 2 and w_ada.shape[0] == DEPTH
    ret_dk = d // RET_HEADS
    ret_dv = 2 * d // RET_HEADS
    moba_dh = d // MOBA_HEADS
    moba_half = moba_dh // 8

    mod = _ada(c, w_ada, b_ada)
    shift, scale, gate = [[mod[l, :, k * d:(k + 1) * d].reshape(b, 1, d) for l in range(DEPTH)]
                          for k in range(3)]
    cos_r, sin_r, mc, ms = _rope_tables(positions, ret_dk, 2 * moba_half)

    x2d = x.reshape(b * s, d)
    n_qk = 2 * RET_HEADS * ret_dk
    qk, h2d = _inproj_qk(x2d, shift[0], scale[0], ret_w_in[0, :, :n_qk].astype(BF16),
                         cos_r.reshape(b * s, LANES), sin_r.reshape(b * s, LANES),
                         seq=s, head_w=ret_dk, k_mult=ret_dk ** -0.5)
    vg = _inproj_plain(h2d, ret_w_in[0], col0=n_qk)
    u = _retention_core(qk.reshape(b, s, -1), vg.reshape(b, s, -1),
                        heads=RET_HEADS, dk=ret_dk, dv=ret_dv)
    x2d, h2d = _outproj_ln(u.reshape(b * s, -1), ret_w_out[0].astype(BF16), x2d, gate[0],
                           ln_g[0], ln_b[0], next_mod=(shift[1], scale[1]), seq=s, tm=256)

    n_qk = 2 * MOBA_HEADS * moba_dh
    qk = _inproj_plain(h2d, _moba_w_qk_bf16(moba_w_in[0], n_qk, moba_dh, moba_half))
    vg = _inproj_plain(h2d, moba_w_in[0], col0=n_qk)
    u = _moba_core(qk.reshape(b, s, -1), vg.reshape(b, s, -1), mc, ms,
                   heads=MOBA_HEADS, dh=moba_dh)
    x2d = _outproj_ln(u.reshape(b * s, -1), moba_w_out[0], x2d, gate[1],
                      ln_g[1], ln_b[1], seq=s, tm=512)
    return x2d.reshape(b, s, d)
```

```python
import functools
import math

import jax
import jax.numpy as jnp
import numpy as np
from jax import lax
from jax.experimental import pallas as pl
from jax.experimental.pallas import tpu as pltpu

F32 = jnp.float32
BF16 = jnp.bfloat16

DEPTH = 2
N_MIXERS = 2
DEEPNORM_ALPHA = (2.0 * DEPTH) ** 0.25
LN_EPS = 1e-5

RET_HEADS = 8
RET_CHUNK = 256
RET_THETA = 10000.0
RET_GN_EPS = 1e-5

MOBA_HEADS = 16
MOBA_BLOCK = 256
MOBA_TOPK = 3
MOBA_SCORE_ROWS = 256
ROPE_THETA = 500000.0
NEG = -1e30

LANES = 128
BF16_SUBLANES = 16
VMEM_LIMIT = 56 * 1024 * 1024


def _params(sem):
    return pltpu.CompilerParams(dimension_semantics=sem, vmem_limit_bytes=VMEM_LIMIT)


def _setup_kernel(c_ref, wada_ref, bada_ref, pos_ref, inv_r_ref, inv_m_ref,
                  rwin_ref, rwout_ref, mwin_ref,
                  mod_ref, cos_r, sin_r, mc, ms, rwin_o, rwout_o, mwin_o, *, half_m):
    acc = jnp.dot(c_ref[...].astype(BF16), wada_ref[...].astype(BF16),
                  preferred_element_type=F32)
    mod_ref[...] = acc + bada_ref[...]

    rwin_o[...] = rwin_ref[...].astype(rwin_o.dtype)
    rwout_o[...] = rwout_ref[...].astype(rwout_o.dtype)

    mid = LANES // 2
    x = mwin_ref[...]
    wlane = lax.broadcasted_iota(jnp.int32, x.shape, 1)
    up = pltpu.roll(x, mid - half_m, 1)
    dn = pltpu.roll(x, LANES - (mid - half_m), 1)
    y = jnp.where((wlane >= half_m) & (wlane < 2 * half_m), dn,
                  jnp.where((wlane >= mid) & (wlane < mid + half_m), up, x))
    mwin_o[...] = y.astype(mwin_o.dtype)

    pos = pos_ref[0].astype(F32)
    ang = pos * inv_r_ref[...]
    cos_r[0] = jnp.cos(ang)
    sin_r[0] = jnp.sin(ang)
    angm = pos * inv_m_ref[...]
    cm = jnp.cos(angm)
    sm = jnp.sin(angm)
    lane = lax.broadcasted_iota(jnp.int32, angm.shape, 1)
    first = lane < half_m
    second = (lane >= LANES // 2) & (lane < LANES // 2 + half_m)
    mc[0] = jnp.where(first | second, cm, 1.0)
    ms[0] = jnp.where(first, -sm, jnp.where(second, sm, 0.0))


def _setup(c, w_ada, b_ada, positions, ret_w_in, ret_w_out, moba_w_in, *,
           ret_dk, ret_qk, moba_dh, moba_qk, ts=512):
    depth, d, n_mod = w_ada.shape
    b, s = positions.shape
    steps = b * s // ts
    seq_tiles = s // ts
    half_r = ret_dk // 2
    half_m = moba_dh // 8
    assert half_r == LANES and moba_dh == LANES and LANES % half_m == 0
    tn_mod = depth * n_mod // steps
    out_rows = ret_w_out.shape[0] // steps
    assert n_mod % tn_mod == 0 and tn_mod % LANES == 0 and out_rows % 8 == 0
    assert ret_qk == steps * LANES and moba_qk == steps * LANES
    mod_tiles = n_mod // tn_mod

    inv_r = RET_THETA ** (-jnp.arange(half_r, dtype=F32) * 2.0 / ret_dk)
    inv_m = ROPE_THETA ** (-jnp.arange(half_m, dtype=F32) * 2.0 / (2 * half_m))
    inv_m = jnp.tile(inv_m, LANES // half_m)

    row = pl.BlockSpec((1, LANES), lambda t: (0, 0))
    mod_idx = lambda t: (t // mod_tiles, 0, t % mod_tiles)
    tab_idx = lambda t: (t // seq_tiles, t % seq_tiles, 0)
    col_tile = pl.BlockSpec((d, LANES), lambda t: (0, t))
    row_tile = pl.BlockSpec((out_rows, ret_w_out.shape[1]), lambda t: (t, 0))
    tab = jax.ShapeDtypeStruct((b, s, LANES), F32)
    outs = pl.pallas_call(
        functools.partial(_setup_kernel, half_m=half_m),
        grid=(steps,),
        in_specs=[
            pl.BlockSpec((b, d), lambda t: (0, 0)),
            pl.BlockSpec((None, d, tn_mod), mod_idx),
            pl.BlockSpec((None, 1, tn_mod), mod_idx),
            pl.BlockSpec((1, ts, 1), tab_idx), row, row,
            col_tile, row_tile, col_tile,
        ],
        out_specs=[pl.BlockSpec((None, b, tn_mod), mod_idx)]
        + [pl.BlockSpec((1, ts, LANES), tab_idx)] * 4
        + [col_tile, row_tile, col_tile],
        out_shape=[jax.ShapeDtypeStruct((depth, b, n_mod), F32), tab, tab, tab, tab,
                   jax.ShapeDtypeStruct((d, ret_qk), BF16),
                   jax.ShapeDtypeStruct(ret_w_out.shape, BF16),
                   jax.ShapeDtypeStruct((d, moba_qk), BF16)],
        compiler_params=_params(("parallel",)),
        name="setup",
    )(c, w_ada, b_ada.reshape(depth, 1, n_mod), positions.reshape(b, s, 1),
      inv_r.reshape(1, LANES), inv_m.reshape(1, LANES), ret_w_in, ret_w_out, moba_w_in)
    return outs[0], outs[1:5], outs[5], outs[6], outs[7]


def _inproj_qk_kernel(x_ref, shift_ref, scale_ref, w_ref, cos_ref, sin_ref, o_ref, h_ref,
                      *, tn, head_w, k_mult):
    h = x_ref[...] * (1.0 + scale_ref[0]) + shift_ref[0]
    h_ref[...] = h.astype(h_ref.dtype)
    n = w_ref.shape[1]
    cos_t, sin_t = cos_ref[...], sin_ref[...]
    for c0 in range(0, n, tn):
        acc = jnp.dot(h_ref[...], w_ref[:, c0:c0 + tn], preferred_element_type=F32)
        is_q = c0 < n // 2
        c = cos_t if is_q else cos_t * k_mult
        s = sin_t if is_q else sin_t * k_mult
        for lo in range(0, tn, head_w):
            x1 = acc[:, lo:lo + LANES]
            x2 = acc[:, lo + LANES:lo + 2 * LANES]
            o_ref[:, c0 + lo:c0 + lo + LANES] = (x1 * c - x2 * s).astype(o_ref.dtype)
            o_ref[:, c0 + lo + LANES:c0 + lo + 2 * LANES] = (x2 * c + x1 * s).astype(o_ref.dtype)


def _inproj_qk(x2d, shift, scale, w_qk, cos_t, sin_t, *, seq, head_w, k_mult, tm=512, tn=1024):
    m, d = x2d.shape
    n = w_qk.shape[1]
    assert seq % tm == 0 and (n // 2) % tn == 0 and tn % head_w == 0 and head_w == 2 * LANES
    tiles_per_seq = seq // tm
    tab_spec = pl.BlockSpec((tm, LANES), lambda i: (i, 0))
    mod_spec = pl.BlockSpec((1, 1, d), lambda i: (i // tiles_per_seq, 0, 0))
    return pl.pallas_call(
        functools.partial(_inproj_qk_kernel, tn=tn, head_w=head_w, k_mult=k_mult),
        grid=(m // tm,),
        in_specs=[
            pl.BlockSpec((tm, d), lambda i: (i, 0)),
            mod_spec, mod_spec,
            pl.BlockSpec((d, n), lambda i: (0, 0), pipeline_mode=pl.Buffered(1)),
            tab_spec, tab_spec,
        ],
        out_specs=[pl.BlockSpec((tm, n), lambda i: (i, 0)),
                   pl.BlockSpec((tm, d), lambda i: (i, 0))],
        out_shape=[jax.ShapeDtypeStruct((m, n), BF16), jax.ShapeDtypeStruct((m, d), BF16)],
        compiler_params=_params(("parallel",)),
        name="inproj_qk",
    )(x2d, shift, scale, w_qk, cos_t, sin_t)


def _matmul_kernel(h_ref, w_ref, o_ref):
    o_ref[...] = jnp.dot(h_ref[...], w_ref[...].astype(h_ref.dtype),
                         preferred_element_type=F32).astype(o_ref.dtype)


def _inproj_plain(h2d, w_mat, *, col0=0, tm=2048, tn=1024):
    m, d = h2d.shape
    n = w_mat.shape[1] - col0
    assert m % tm == 0 and n % tn == 0 and col0 % tn == 0
    j0 = col0 // tn
    return pl.pallas_call(
        _matmul_kernel,
        grid=(m // tm, n // tn),
        in_specs=[pl.BlockSpec((tm, d), lambda i, j: (i, 0)),
                  pl.BlockSpec((d, tn), lambda i, j: (0, j0 + j))],
        out_specs=pl.BlockSpec((tm, tn), lambda i, j: (i, j)),
        out_shape=jax.ShapeDtypeStruct((m, n), BF16),
        compiler_params=_params(("parallel", "parallel")),
        name="inproj_plain",
    )(h2d, w_mat)


def _silu(g):
    half = 0.5 * g
    return half + half * jnp.tanh(half)


def _retention_kernel(q_ref, k_ref, v_ref, g_ref, dec_ref, qd_ref, kd_ref, cd_ref, o_ref, st_scrs,
                      *, n_chunks, chunk, dk, dv):
    n_heads = q_ref.shape[2] // dk
    programs = [_retention_head_program(hh, q_ref, k_ref, v_ref, g_ref, dec_ref.at[hh],
                                        qd_ref.at[hh], kd_ref.at[hh], cd_ref.at[hh], o_ref,
                                        st_scrs.at[hh], n_chunks=n_chunks, chunk=chunk,
                                        dk=dk, dv=dv)
                for hh in range(n_heads)]
    while programs:
        programs = [p for p in programs if next(p, "done") != "done"]


def _retention_head_program(hh, q_ref, k_ref, v_ref, g_ref, dec_ref, qd_ref, kd_ref, cd_ref,
                            o_ref, st_scr, *, n_chunks, chunk, dk, dv):
    qk_cols = slice(hh * dk, (hh + 1) * dk)
    vg_cols = slice(hh * dv, (hh + 1) * dv)
    st_scr[...] = jnp.zeros_like(st_scr)
    qd = qd_ref[...]
    cd = cd_ref[...]

    def state_free(ci):
        rows = slice(ci * chunk, (ci + 1) * chunk)
        qc = q_ref[0, rows, qk_cols]
        kc = k_ref[0, rows, qk_cols]
        vc = v_ref[0, rows, vg_cols]
        scores = lax.dot_general(qc, kc, (((1,), (1,)), ((), ())),
                                 preferred_element_type=F32) * dec_ref[...]
        inner = jnp.dot(scores.astype(BF16), vc, preferred_element_type=F32)
        kdec = (kc.astype(F32) * kd_ref[...]).astype(BF16)
        update = lax.dot_general(kdec, vc, (((0,), (0,)), ((), ())), preferred_element_type=F32)
        return inner, update

    def finish(ci, inner, update):
        rows = slice(ci * chunk, (ci + 1) * chunk)
        state = st_scr[...]
        cross = jnp.dot(q_ref[0, rows, qk_cols], state.astype(BF16),
                        preferred_element_type=F32) * qd
        st_scr[...] = state * cd + update
        o = inner + cross
        mu = jnp.mean(o, axis=-1, keepdims=True)
        oc = o - mu
        var = jnp.mean(oc * oc, axis=-1, keepdims=True)
        gn = oc * lax.rsqrt(var + RET_GN_EPS)
        g = g_ref[0, rows, vg_cols].astype(F32)
        o_ref[0, rows, vg_cols] = (gn * _silu(g)).astype(o_ref.dtype)

    nxt = state_free(0)
    yield
    for ci in range(n_chunks):
        cur = nxt
        if ci + 1 < n_chunks:
            nxt = state_free(ci + 1)
            yield
        finish(ci, *cur)
        yield


def _retention_decays(heads, chunk):
    log_gamma = jnp.log(1.0 - 2.0 ** (-5.0 - jnp.arange(heads, dtype=F32)))
    n = jnp.arange(chunk, dtype=F32)
    diff = n[:, None] - n[None, :]
    inner = jnp.where(diff >= 0, jnp.exp(jnp.maximum(diff, 0.0) * log_gamma[:, None, None]), 0.0)
    q_decay = jnp.exp((n + 1.0) * log_gamma[:, None])
    k_decay = jnp.exp((chunk - 1.0 - n) * log_gamma[:, None])
    chunk_decay = jnp.exp(chunk * log_gamma)
    return (inner.astype(F32), q_decay.reshape(heads, chunk, 1), k_decay.reshape(heads, chunk, 1),
            chunk_decay.reshape(heads, 1, 1))


def _retention_core(qk, vg, *, heads, dk, dv, heads_per_step=2):
    b, s, _ = qk.shape
    chunk = RET_CHUNK
    hp = heads_per_step
    assert s % chunk == 0 and heads % hp == 0
    groups = heads // hp
    dec, qd, kd, cd = _retention_decays(heads, chunk)
    return pl.pallas_call(
        functools.partial(_retention_kernel, n_chunks=s // chunk, chunk=chunk, dk=dk, dv=dv),
        grid=(b, groups),
        in_specs=[
            pl.BlockSpec((1, s, hp * dk), lambda i, h: (i, 0, h)),
            pl.BlockSpec((1, s, hp * dk), lambda i, h: (i, 0, groups + h)),
            pl.BlockSpec((1, s, hp * dv), lambda i, h: (i, 0, h)),
            pl.BlockSpec((1, s, hp * dv), lambda i, h: (i, 0, groups + h)),
            pl.BlockSpec((hp, chunk, chunk), lambda i, h: (h, 0, 0)),
            pl.BlockSpec((hp, chunk, 1), lambda i, h: (h, 0, 0)),
            pl.BlockSpec((hp, chunk, 1), lambda i, h: (h, 0, 0)),
            pl.BlockSpec((hp, 1, 1), lambda i, h: (h, 0, 0)),
        ],
        out_specs=pl.BlockSpec((1, s, hp * dv), lambda i, h: (i, 0, h)),
        out_shape=jax.ShapeDtypeStruct((b, s, heads * dv), BF16),
        scratch_shapes=[pltpu.VMEM((hp, dk, dv), F32)],
        compiler_params=_params(("parallel", "parallel")),
        name="retention_core",
    )(qk, qk, vg, vg, dec, qd, kd, cd)


def _moba_kernel(q_ref, k_ref, v_ref, g_ref, mc_ref, ms_ref, o_ref,
                 qr_scrs, kr_scrs, vt_scrs, km_scrs, *s_scrs,
                 n_blocks, blk, sub, topk, exp_scale, dh):
    n_heads = q_ref.shape[2] // dh
    programs = [_moba_head_program(hh, q_ref, k_ref, v_ref, g_ref, mc_ref, ms_ref, o_ref,
                                   qr_scrs.at[hh], kr_scrs.at[hh], vt_scrs.at[hh], km_scrs.at[hh],
                                   s_scrs[hh * n_blocks:(hh + 1) * n_blocks],
                                   n_blocks=n_blocks, blk=blk, sub=sub, topk=topk,
                                   exp_scale=exp_scale, dh=dh)
                for hh in range(n_heads)]
    while programs:
        programs = [p for p in programs if next(p, "done") != "done"]


def _moba_head_program(hh, q_ref, k_ref, v_ref, g_ref, mc_ref, ms_ref, o_ref,
                       qr_scr, kr_scr, vt_scr, km_scr, s_scrs, *,
                       n_blocks, blk, sub, topk, exp_scale, dh):
    lanes = slice(hh * dh, (hh + 1) * dh)
    vt_scr[dh:, :] = jnp.ones((vt_scr.shape[0] - dh, vt_scr.shape[1]), BF16)
    km_scr[...] = jnp.zeros_like(km_scr)

    def prepare(b):
        rows = slice(b * blk, (b + 1) * blk)
        mc = mc_ref[0, rows, :]
        ms = ms_ref[0, rows, :]

        def rotary(x_ref):
            x = x_ref[0, rows, lanes].astype(F32)
            return x * mc + pltpu.roll(x, LANES // 2, 1) * ms

        qr_scr[rows, :] = (rotary(q_ref) * exp_scale).astype(BF16)
        kr = rotary(k_ref)
        kr_scr[rows, :] = kr.astype(BF16)
        km_scr[b:b + 1, :] = jnp.mean(kr, axis=0, keepdims=True)
        vt_scr[:dh, rows] = v_ref[0, rows, lanes].T

    nt = (((1,), (1,)), ((), ()))
    key_id = lax.broadcasted_iota(jnp.int32, (sub, blk), 0)
    qry_id = lax.broadcasted_iota(jnp.int32, (sub, blk), 1)
    blk_id = lax.broadcasted_iota(jnp.int32, (n_blocks, blk), 0)

    def scores(i):
        cols = slice(i * blk, (i + 1) * blk)
        s_scr = s_scrs[i]
        qi = qr_scr[cols, :]
        ranked = i > topk
        if ranked:
            gi = lax.dot_general(km_scr[...].astype(BF16), qi, nt,
                                 preferred_element_type=F32)
            past = blk_id < i
        m = None
        bias = []
        for j in range(i + 1):
            mj = None
            for part in range(blk // sub):
                keys = slice(j * blk + part * sub, j * blk + (part + 1) * sub)
                sj = lax.dot_general(kr_scr[keys, :], qi, nt,
                                     preferred_element_type=F32)
                if j == i:
                    sj = jnp.where(key_id + part * sub <= qry_id, sj, NEG)
                s_scr[keys, :] = sj
                mp = jnp.max(sj, axis=0, keepdims=True)
                mj = mp if mj is None else jnp.maximum(mj, mp)
            if ranked and j < i:
                gj = gi[j:j + 1, :]
                beats = past & ((gi > gj) | ((gi == gj) & (blk_id < j)))
                cnt = jnp.sum(jnp.where(beats, 1.0, 0.0), axis=0, keepdims=True)
                bias.append(jnp.where(cnt < topk, 0.0, NEG))
                mj = mj + bias[-1]
            m = mj if m is None else jnp.maximum(m, mj)
        if not ranked:
            return [m]
        return [m - bj for bj in bias] + [m]

    def attend(i, subs):
        cols = slice(i * blk, (i + 1) * blk)
        ot = None
        for j in range(i + 1):
            keys = slice(j * blk, (j + 1) * blk)
            e = jnp.exp2(s_scrs[i][keys, :] - subs[min(j, len(subs) - 1)]).astype(BF16)
            pj = jnp.dot(vt_scr[:, keys], e, preferred_element_type=F32)
            ot = pj if ot is None else pj + ot
        o = (ot[:dh, :] * (1.0 / ot[dh:dh + 1, :])).T
        g = g_ref[0, cols, lanes].astype(F32)
        o_ref[0, cols, lanes] = (o * _silu(g)).astype(o_ref.dtype)

    order = list(range(1, n_blocks, 2)) + list(range((n_blocks - 1) // 2 * 2, -1, -2))
    prepared = 0

    def prepare_upto(b):
        nonlocal prepared
        for blk_i in range(prepared, b + 1):
            prepare(blk_i)
        prepared = max(prepared, b + 1)

    prepare_upto(order[0])
    yield
    nxt = scores(order[0])
    yield
    for pos, i in enumerate(order):
        cur = nxt
        if pos + 2 < n_blocks:
            prepare_upto(order[pos + 2])
        if pos + 1 < n_blocks:
            prepare_upto(order[pos + 1])
            nxt = scores(order[pos + 1])
            yield
        attend(i, cur)
        yield


def _moba_core(qk, vg, mc, ms, *, heads, dh, heads_per_step=2):
    b, s, _ = qk.shape
    blk = MOBA_BLOCK
    hp = heads_per_step
    assert s % blk == 0 and dh == LANES and heads % hp == 0
    n_blocks = s // blk
    groups = heads // hp
    spec = lambda off: pl.BlockSpec((1, s, hp * dh), lambda i, h: (i, 0, off + h))
    tab = pl.BlockSpec((1, s, dh), lambda i, h: (i, 0, 0))
    return pl.pallas_call(
        functools.partial(_moba_kernel, n_blocks=n_blocks, blk=blk, sub=MOBA_SCORE_ROWS,
                          topk=min(MOBA_TOPK, n_blocks),
                          exp_scale=dh ** -0.5 * math.log2(math.e), dh=dh),
        grid=(b, groups),
        in_specs=[spec(0), spec(groups), spec(0), spec(groups), tab, tab],
        out_specs=spec(0),
        out_shape=jax.ShapeDtypeStruct((b, s, heads * dh), BF16),
        scratch_shapes=[pltpu.VMEM((hp, s, dh), BF16), pltpu.VMEM((hp, s, dh), BF16),
                        pltpu.VMEM((hp, dh + BF16_SUBLANES, s), BF16),
                        pltpu.VMEM((hp, n_blocks, dh), F32)]
        + [pltpu.VMEM(((i + 1) * blk, blk), F32) for _ in range(hp) for i in range(n_blocks)],
        compiler_params=_params(("parallel", "parallel")),
        name="moba_core",
    )(qk, qk, vg, vg, mc, ms)


def _outproj_ln_kernel(u_ref, w_ref, x_ref, gate_ref, lng_ref, lnb_ref, *rest, sub, emit_h):
    if emit_h:
        nshift_ref, nscale_ref, o_ref, h_ref = rest
    else:
        (o_ref,) = rest
    n_sub = u_ref.shape[0] // sub

    def matmul(r):
        return jnp.dot(u_ref[r * sub:(r + 1) * sub, :], w_ref[...].astype(u_ref.dtype),
                       preferred_element_type=F32)

    y_next = matmul(0)
    for r in range(n_sub):
        rows = slice(r * sub, (r + 1) * sub)
        y = y_next
        if r + 1 < n_sub:
            y_next = matmul(r + 1)
        o_ref[rows, :] = DEEPNORM_ALPHA * x_ref[rows, :] + gate_ref[0] * y
        mu = jnp.mean(o_ref[rows, :], axis=-1, keepdims=True)
        var = jnp.mean(jnp.square(o_ref[rows, :] - mu), axis=-1, keepdims=True)
        xn = (o_ref[rows, :] - mu) * lax.rsqrt(var + LN_EPS) * lng_ref[...] + lnb_ref[...]
        o_ref[rows, :] = xn
        if emit_h:
            h_ref[rows, :] = (xn * (1.0 + nscale_ref[0]) + nshift_ref[0]).astype(h_ref.dtype)


def _outproj_ln(u2d, w_mat, x2d, gate, ln_g, ln_b, next_mod=None, *, seq, tm, sub=256):
    m, kdim = u2d.shape
    d = w_mat.shape[1]
    assert seq % tm == 0 and tm % sub == 0
    tiles_per_seq = seq // tm
    row = pl.BlockSpec((1, d), lambda i: (0, 0))
    mod_spec = pl.BlockSpec((1, 1, d), lambda i: (i // tiles_per_seq, 0, 0))
    tile = pl.BlockSpec((tm, d), lambda i: (i, 0))
    emit_h = next_mod is not None
    in_specs = [
        pl.BlockSpec((tm, kdim), lambda i: (i, 0)),
        pl.BlockSpec((kdim, d), lambda i: (0, 0), pipeline_mode=pl.Buffered(1)),
        tile, mod_spec, row, row,
    ]
    args = [u2d, w_mat, x2d, gate, ln_g.reshape(1, d), ln_b.reshape(1, d)]
    out_specs, out_shape = tile, jax.ShapeDtypeStruct((m, d), F32)
    if emit_h:
        in_specs += [mod_spec, mod_spec]
        args += list(next_mod)
        out_specs = [tile, tile]
        out_shape = [out_shape, jax.ShapeDtypeStruct((m, d), BF16)]
    return pl.pallas_call(
        functools.partial(_outproj_ln_kernel, sub=sub, emit_h=emit_h),
        grid=(m // tm,),
        in_specs=in_specs,
        out_specs=out_specs,
        out_shape=out_shape,
        compiler_params=_params(("parallel",)),
        name="outproj_ln",
    )(*args)


def kernel(x, c, positions, ret_w_in, ret_w_out, moba_w_in, moba_w_out, w_ada, b_ada, ln_g, ln_b):
    b, s, d = x.shape
    assert DEPTH == 2 and w_ada.shape[0] == DEPTH
    ret_dk = d // RET_HEADS
    ret_dv = 2 * d // RET_HEADS
    moba_dh = d // MOBA_HEADS
    ret_qk = 2 * RET_HEADS * ret_dk
    moba_qk = 2 * MOBA_HEADS * moba_dh

    mod, (cos_r, sin_r, mc, ms), ret_w_qk, ret_w_o, moba_w_qk = _setup(
        c, w_ada, b_ada, positions, ret_w_in[0], ret_w_out[0], moba_w_in[0],
        ret_dk=ret_dk, ret_qk=ret_qk, moba_dh=moba_dh, moba_qk=moba_qk)
    shift, scale, gate = [[mod[l, :, k * d:(k + 1) * d].reshape(b, 1, d) for l in range(DEPTH)]
                          for k in range(3)]

    x2d = x.reshape(b * s, d)
    qk, h2d = _inproj_qk(x2d, shift[0], scale[0], ret_w_qk,
                         cos_r.reshape(b * s, LANES), sin_r.reshape(b * s, LANES),
                         seq=s, head_w=ret_dk, k_mult=ret_dk ** -0.5)
    vg = _inproj_plain(h2d, ret_w_in[0], col0=ret_qk)
    u = _retention_core(qk.reshape(b, s, -1), vg.reshape(b, s, -1),
                        heads=RET_HEADS, dk=ret_dk, dv=ret_dv)
    x2d, h2d = _outproj_ln(u.reshape(b * s, -1), ret_w_o, x2d, gate[0],
                           ln_g[0], ln_b[0], next_mod=(shift[1], scale[1]), seq=s, tm=256)

    qk = _inproj_plain(h2d, moba_w_qk)
    vg = _inproj_plain(h2d, moba_w_in[0], col0=moba_qk)
    u = _moba_core(qk.reshape(b, s, -1), vg.reshape(b, s, -1), mc, ms,
                   heads=MOBA_HEADS, dh=moba_dh)
    x2d = _outproj_ln(u.reshape(b * s, -1), moba_w_out[0], x2d, gate[1],
                      ln_g[1], ln_b[1], seq=s, tm=512)
    return x2d.reshape(b, s, d)
```

```python
import functools
import math

import jax
import jax.numpy as jnp
import numpy as np
from jax import lax
from jax.experimental import pallas as pl
from jax.experimental.pallas import tpu as pltpu

F32 = jnp.float32
BF16 = jnp.bfloat16

DEPTH = 2
N_MIXERS = 2
DEEPNORM_ALPHA = (2.0 * DEPTH) ** 0.25
LN_EPS = 1e-5

RET_HEADS = 8
RET_CHUNK = 256
RET_THETA = 10000.0
RET_GN_EPS = 1e-5

MOBA_HEADS = 16
MOBA_BLOCK = 256
MOBA_TOPK = 3
MOBA_SCORE_ROWS = 256
ROPE_THETA = 500000.0
NEG = -1e30

LANES = 128
BF16_SUBLANES = 16
VMEM_LIMIT = 56 * 1024 * 1024


def _params(sem):
    return pltpu.CompilerParams(dimension_semantics=sem, vmem_limit_bytes=VMEM_LIMIT)


def _setup_kernel(c_ref, wada_ref, bada_ref, pos_ref, inv_r_ref, inv_m_ref,
                  rwin_ref, rwout_ref, mwin_ref,
                  mod_ref, cos_r, sin_r, mc, ms, rwin_o, rwout_o, mwin_o, *, half_m):
    acc = jnp.dot(c_ref[...].astype(BF16), wada_ref[...].astype(BF16),
                  preferred_element_type=F32)
    mod_ref[...] = acc + bada_ref[...]

    rwin_o[...] = rwin_ref[...].astype(rwin_o.dtype)
    rwout_o[...] = rwout_ref[...].astype(rwout_o.dtype)

    mid = LANES // 2
    x = mwin_ref[...]
    wlane = lax.broadcasted_iota(jnp.int32, x.shape, 1)
    up = pltpu.roll(x, mid - half_m, 1)
    dn = pltpu.roll(x, LANES - (mid - half_m), 1)
    y = jnp.where((wlane >= half_m) & (wlane < 2 * half_m), dn,
                  jnp.where((wlane >= mid) & (wlane < mid + half_m), up, x))
    mwin_o[...] = y.astype(mwin_o.dtype)

    pos = pos_ref[0].astype(F32)
    ang = pos * inv_r_ref[...]
    cos_r[0] = jnp.cos(ang)
    sin_r[0] = jnp.sin(ang)
    angm = pos * inv_m_ref[...]
    cm = jnp.cos(angm)
    sm = jnp.sin(angm)
    lane = lax.broadcasted_iota(jnp.int32, angm.shape, 1)
    first = lane < half_m
    second = (lane >= LANES // 2) & (lane < LANES // 2 + half_m)
    mc[0] = jnp.where(first | second, cm, 1.0)
    ms[0] = jnp.where(first, -sm, jnp.where(second, sm, 0.0))


def _setup(c, w_ada, b_ada, positions, ret_w_in, ret_w_out, moba_w_in, *,
           ret_dk, ret_qk, moba_dh, moba_qk, ts=512):
    depth, d, n_mod = w_ada.shape
    b, s = positions.shape
    steps = b * s // ts
    seq_tiles = s // ts
    half_r = ret_dk // 2
    half_m = moba_dh // 8
    assert half_r == LANES and moba_dh == LANES and LANES % half_m == 0
    tn_mod = depth * n_mod // steps
    out_rows = ret_w_out.shape[0] // steps
    assert n_mod % tn_mod == 0 and tn_mod % LANES == 0 and out_rows % 8 == 0
    assert ret_qk == steps * LANES and moba_qk == steps * LANES
    mod_tiles = n_mod // tn_mod

    inv_r = RET_THETA ** (-jnp.arange(half_r, dtype=F32) * 2.0 / ret_dk)
    inv_m = ROPE_THETA ** (-jnp.arange(half_m, dtype=F32) * 2.0 / (2 * half_m))
    inv_m = jnp.tile(inv_m, LANES // half_m)

    row = pl.BlockSpec((1, LANES), lambda t: (0, 0))
    mod_idx = lambda t: (t // mod_tiles, 0, t % mod_tiles)
    tab_idx = lambda t: (t // seq_tiles, t % seq_tiles, 0)
    col_tile = pl.BlockSpec((d, LANES), lambda t: (0, t))
    row_tile = pl.BlockSpec((out_rows, ret_w_out.shape[1]), lambda t: (t, 0))
    tab = jax.ShapeDtypeStruct((b, s, LANES), F32)
    outs = pl.pallas_call(
        functools.partial(_setup_kernel, half_m=half_m),
        grid=(steps,),
        in_specs=[
            pl.BlockSpec((b, d), lambda t: (0, 0)),
            pl.BlockSpec((None, d, tn_mod), mod_idx),
            pl.BlockSpec((None, 1, tn_mod), mod_idx),
            pl.BlockSpec((1, ts, 1), tab_idx), row, row,
            col_tile, row_tile, col_tile,
        ],
        out_specs=[pl.BlockSpec((None, b, tn_mod), mod_idx)]
        + [pl.BlockSpec((1, ts, LANES), tab_idx)] * 4
        + [col_tile, row_tile, col_tile],
        out_shape=[jax.ShapeDtypeStruct((depth, b, n_mod), F32), tab, tab, tab, tab,
                   jax.ShapeDtypeStruct((d, ret_qk), BF16),
                   jax.ShapeDtypeStruct(ret_w_out.shape, BF16),
                   jax.ShapeDtypeStruct((d, moba_qk), BF16)],
        compiler_params=_params(("parallel",)),
        name="setup",
    )(c, w_ada, b_ada.reshape(depth, 1, n_mod), positions.reshape(b, s, 1),
      inv_r.reshape(1, LANES), inv_m.reshape(1, LANES), ret_w_in, ret_w_out, moba_w_in)
    return outs[0], outs[1:5], outs[5], outs[6], outs[7]


def _inproj_qk_kernel(x_ref, shift_ref, scale_ref, w_ref, cos_ref, sin_ref, o_ref, h_ref,
                      *, tn, head_w, k_mult):
    h = x_ref[...] * (1.0 + scale_ref[0]) + shift_ref[0]
    h_ref[...] = h.astype(h_ref.dtype)
    n = w_ref.shape[1]
    cos_t, sin_t = cos_ref[...], sin_ref[...]
    for c0 in range(0, n, tn):
        acc = jnp.dot(h_ref[...], w_ref[:, c0:c0 + tn], preferred_element_type=F32)
        is_q = c0 < n // 2
        c = cos_t if is_q else cos_t * k_mult
        s = sin_t if is_q else sin_t * k_mult
        for lo in range(0, tn, head_w):
            x1 = acc[:, lo:lo + LANES]
            x2 = acc[:, lo + LANES:lo + 2 * LANES]
            o_ref[:, c0 + lo:c0 + lo + LANES] = (x1 * c - x2 * s).astype(o_ref.dtype)
            o_ref[:, c0 + lo + LANES:c0 + lo + 2 * LANES] = (x2 * c + x1 * s).astype(o_ref.dtype)


def _inproj_qk(x2d, shift, scale, w_qk, cos_t, sin_t, *, seq, head_w, k_mult, tm=512, tn=1024):
    m, d = x2d.shape
    n = w_qk.shape[1]
    assert seq % tm == 0 and (n // 2) % tn == 0 and tn % head_w == 0 and head_w == 2 * LANES
    tiles_per_seq = seq // tm
    tab_spec = pl.BlockSpec((tm, LANES), lambda i: (i, 0))
    mod_spec = pl.BlockSpec((1, 1, d), lambda i: (i // tiles_per_seq, 0, 0))
    return pl.pallas_call(
        functools.partial(_inproj_qk_kernel, tn=tn, head_w=head_w, k_mult=k_mult),
        grid=(m // tm,),
        in_specs=[
            pl.BlockSpec((tm, d), lambda i: (i, 0)),
            mod_spec, mod_spec,
            pl.BlockSpec((d, n), lambda i: (0, 0), pipeline_mode=pl.Buffered(1)),
            tab_spec, tab_spec,
        ],
        out_specs=[pl.BlockSpec((tm, n), lambda i: (i, 0)),
                   pl.BlockSpec((tm, d), lambda i: (i, 0))],
        out_shape=[jax.ShapeDtypeStruct((m, n), BF16), jax.ShapeDtypeStruct((m, d), BF16)],
        compiler_params=_params(("parallel",)),
        name="inproj_qk",
    )(x2d, shift, scale, w_qk, cos_t, sin_t)


def _matmul_kernel(h_ref, w_ref, o_ref):
    o_ref[...] = jnp.dot(h_ref[...], w_ref[...].astype(h_ref.dtype),
                         preferred_element_type=F32).astype(o_ref.dtype)


def _inproj_plain(h2d, w_mat, *, col0=0, tm=2048, tn=1024):
    m, d = h2d.shape
    n = w_mat.shape[1] - col0
    assert m % tm == 0 and n % tn == 0 and col0 % tn == 0
    j0 = col0 // tn
    return pl.pallas_call(
        _matmul_kernel,
        grid=(m // tm, n // tn),
        in_specs=[pl.BlockSpec((tm, d), lambda i, j: (i, 0)),
                  pl.BlockSpec((d, tn), lambda i, j: (0, j0 + j))],
        out_specs=pl.BlockSpec((tm, tn), lambda i, j: (i, j)),
        out_shape=jax.ShapeDtypeStruct((m, n), BF16),
        compiler_params=_params(("parallel", "parallel")),
        name="inproj_plain",
    )(h2d, w_mat)


def _silu(g):
    half = 0.5 * g
    return half + half * jnp.tanh(half)


def _layernorm_via(o_ref, rows, z, lng, lnb):
    o_ref[rows] = z
    mu = jnp.mean(o_ref[rows], axis=-1, keepdims=True)
    var = jnp.mean(jnp.square(o_ref[rows] - mu), axis=-1, keepdims=True)
    xn = (o_ref[rows] - mu) * lax.rsqrt(var + LN_EPS) * lng + lnb
    o_ref[rows] = xn
    return xn


def _retention_outproj_kernel(q_ref, k_ref, v_ref, g_ref, dec_ref, qd_ref, kd_ref, cd_ref,
                              w_ref, x_ref, gate_ref, lng_ref, lnb_ref, nshift_ref, nscale_ref,
                              o_ref, h_ref, st_scr, *, heads, dk, dv):
    @pl.when(pl.program_id(1) == 0)
    def _():
        st_scr[...] = jnp.zeros_like(st_scr)

    def retain(hh):
        qc = q_ref[0, :, hh * dk:(hh + 1) * dk]
        kc = k_ref[0, :, hh * dk:(hh + 1) * dk]
        vc = v_ref[0, :, hh * dv:(hh + 1) * dv]
        scores = lax.dot_general(qc, kc, (((1,), (1,)), ((), ())),
                                 preferred_element_type=F32) * dec_ref[hh]
        inner = jnp.dot(scores.astype(BF16), vc, preferred_element_type=F32)
        kdec = (kc.astype(F32) * kd_ref[hh]).astype(BF16)
        update = lax.dot_general(kdec, vc, (((0,), (0,)), ((), ())), preferred_element_type=F32)
        state = st_scr[hh]
        cross = jnp.dot(qc, state.astype(BF16), preferred_element_type=F32) * qd_ref[hh]
        st_scr[hh] = state * cd_ref[hh] + update
        return inner + cross

    def gate_norm(hh, o):
        mu = jnp.mean(o, axis=-1, keepdims=True)
        oc = o - mu
        var = jnp.mean(oc * oc, axis=-1, keepdims=True)
        gn = oc * lax.rsqrt(var + RET_GN_EPS)
        g = g_ref[0, :, hh * dv:(hh + 1) * dv].astype(F32)
        return (gn * _silu(g)).astype(BF16)

    ahead = 2
    o = {hh: retain(hh) for hh in range(min(ahead, heads))}
    u = gate_norm(0, o.pop(0))
    y = None
    for hh in range(heads):
        if hh + ahead < heads:
            o[hh + ahead] = retain(hh + ahead)
        part = jnp.dot(u, w_ref[hh * dv:(hh + 1) * dv, :], preferred_element_type=F32)
        y = part if y is None else part + y
        if hh + 1 < heads:
            u = gate_norm(hh + 1, o.pop(hh + 1))

    xn = _layernorm_via(o_ref, 0, DEEPNORM_ALPHA * x_ref[0] + gate_ref[0] * y,
                        lng_ref[...], lnb_ref[...])
    h_ref[0] = (xn * (1.0 + nscale_ref[0]) + nshift_ref[0]).astype(h_ref.dtype)


def _retention_decays(heads, chunk):
    log_gamma = jnp.log(1.0 - 2.0 ** (-5.0 - jnp.arange(heads, dtype=F32)))
    n = jnp.arange(chunk, dtype=F32)
    diff = n[:, None] - n[None, :]
    inner = jnp.where(diff >= 0, jnp.exp(jnp.maximum(diff, 0.0) * log_gamma[:, None, None]), 0.0)
    q_decay = jnp.exp((n + 1.0) * log_gamma[:, None])
    k_decay = jnp.exp((chunk - 1.0 - n) * log_gamma[:, None])
    chunk_decay = jnp.exp(chunk * log_gamma)
    return (inner.astype(F32), q_decay.reshape(heads, chunk, 1), k_decay.reshape(heads, chunk, 1),
            chunk_decay.reshape(heads, 1, 1))


def _retention_outproj(qk, vg, w_out, x, gate, ln_g, ln_b, next_shift, next_scale,
                       *, heads, dk, dv):
    b, s, d = x.shape
    chunk = RET_CHUNK
    assert s % chunk == 0 and w_out.shape == (heads * dv, d)
    dec, qd, kd, cd = _retention_decays(heads, chunk)
    whole = lambda shape: pl.BlockSpec(shape, lambda i, c: (0,) * len(shape))
    rows = lambda width, blk: pl.BlockSpec((1, chunk, width), lambda i, c: (i, c, blk))
    per_batch = pl.BlockSpec((1, 1, d), lambda i, c: (i, 0, 0))
    return pl.pallas_call(
        functools.partial(_retention_outproj_kernel, heads=heads, dk=dk, dv=dv),
        grid=(b, s // chunk),
        in_specs=[
            rows(heads * dk, 0), rows(heads * dk, 1), rows(heads * dv, 0), rows(heads * dv, 1),
            whole((heads, chunk, chunk)), whole((heads, chunk, 1)), whole((heads, chunk, 1)),
            whole((heads, 1, 1)),
            pl.BlockSpec((heads * dv, d), lambda i, c: (0, 0), pipeline_mode=pl.Buffered(1)),
            rows(d, 0), per_batch, whole((1, d)), whole((1, d)), per_batch, per_batch,
        ],
        out_specs=[rows(d, 0), rows(d, 0)],
        out_shape=[jax.ShapeDtypeStruct((b, s, d), F32), jax.ShapeDtypeStruct((b, s, d), BF16)],
        scratch_shapes=[pltpu.VMEM((heads, dk, dv), F32)],
        compiler_params=_params(("parallel", "arbitrary")),
        name="retention_outproj",
    )(qk, qk, vg, vg, dec, qd, kd, cd, w_out, x, gate, ln_g.reshape(1, d), ln_b.reshape(1, d),
      next_shift, next_scale)


def _moba_kernel(q_ref, k_ref, v_ref, g_ref, mc_ref, ms_ref, o_ref,
                 qr_scrs, kr_scrs, vt_scrs, km_scrs, *s_scrs,
                 n_blocks, blk, sub, topk, exp_scale, dh):
    n_heads = q_ref.shape[2] // dh
    programs = [_moba_head_program(hh, q_ref, k_ref, v_ref, g_ref, mc_ref, ms_ref, o_ref,
                                   qr_scrs.at[hh], kr_scrs.at[hh], vt_scrs.at[hh], km_scrs.at[hh],
                                   s_scrs[hh * n_blocks:(hh + 1) * n_blocks],
                                   n_blocks=n_blocks, blk=blk, sub=sub, topk=topk,
                                   exp_scale=exp_scale, dh=dh)
                for hh in range(n_heads)]
    while programs:
        programs = [p for p in programs if next(p, "done") != "done"]


def _moba_head_program(hh, q_ref, k_ref, v_ref, g_ref, mc_ref, ms_ref, o_ref,
                       qr_scr, kr_scr, vt_scr, km_scr, s_scrs, *,
                       n_blocks, blk, sub, topk, exp_scale, dh):
    lanes = slice(hh * dh, (hh + 1) * dh)
    vt_scr[dh:, :] = jnp.ones((vt_scr.shape[0] - dh, vt_scr.shape[1]), BF16)
    km_scr[...] = jnp.zeros_like(km_scr)

    def prepare(b):
        rows = slice(b * blk, (b + 1) * blk)
        mc = mc_ref[0, rows, :]
        ms = ms_ref[0, rows, :]

        def rotary(x_ref):
            x = x_ref[0, rows, lanes].astype(F32)
            return x * mc + pltpu.roll(x, LANES // 2, 1) * ms

        qr_scr[rows, :] = (rotary(q_ref) * exp_scale).astype(BF16)
        kr = rotary(k_ref)
        kr_scr[rows, :] = kr.astype(BF16)
        km_scr[b:b + 1, :] = jnp.mean(kr, axis=0, keepdims=True)
        vt_scr[:dh, rows] = v_ref[0, rows, lanes].T

    nt = (((1,), (1,)), ((), ()))
    key_id = lax.broadcasted_iota(jnp.int32, (sub, blk), 0)
    qry_id = lax.broadcasted_iota(jnp.int32, (sub, blk), 1)
    blk_id = lax.broadcasted_iota(jnp.int32, (n_blocks, blk), 0)

    def scores(i):
        cols = slice(i * blk, (i + 1) * blk)
        s_scr = s_scrs[i]
        qi = qr_scr[cols, :]
        ranked = i > topk
        if ranked:
            gi = lax.dot_general(km_scr[...].astype(BF16), qi, nt,
                                 preferred_element_type=F32)
            past = blk_id < i
        m = None
        bias = []
        for j in range(i + 1):
            mj = None
            for part in range(blk // sub):
                keys = slice(j * blk + part * sub, j * blk + (part + 1) * sub)
                sj = lax.dot_general(kr_scr[keys, :], qi, nt,
                                     preferred_element_type=F32)
                if j == i:
                    sj = jnp.where(key_id + part * sub <= qry_id, sj, NEG)
                s_scr[keys, :] = sj
                mp = jnp.max(sj, axis=0, keepdims=True)
                mj = mp if mj is None else jnp.maximum(mj, mp)
            if ranked and j < i:
                gj = gi[j:j + 1, :]
                beats = past & ((gi > gj) | ((gi == gj) & (blk_id < j)))
                cnt = jnp.sum(jnp.where(beats, 1.0, 0.0), axis=0, keepdims=True)
                bias.append(jnp.where(cnt < topk, 0.0, NEG))
                mj = mj + bias[-1]
            m = mj if m is None else jnp.maximum(m, mj)
        if not ranked:
            return [m]
        return [m - bj for bj in bias] + [m]

    def attend(i, subs):
        cols = slice(i * blk, (i + 1) * blk)
        ot = None
        for j in range(i + 1):
            keys = slice(j * blk, (j + 1) * blk)
            e = jnp.exp2(s_scrs[i][keys, :] - subs[min(j, len(subs) - 1)]).astype(BF16)
            pj = jnp.dot(vt_scr[:, keys], e, preferred_element_type=F32)
            ot = pj if ot is None else pj + ot
        o = (ot[:dh, :] * (1.0 / ot[dh:dh + 1, :])).T
        g = g_ref[0, cols, lanes].astype(F32)
        o_ref[0, cols, lanes] = (o * _silu(g)).astype(o_ref.dtype)

    order = list(range(1, n_blocks, 2)) + list(range((n_blocks - 1) // 2 * 2, -1, -2))
    prepared = 0

    def prepare_upto(b):
        nonlocal prepared
        for blk_i in range(prepared, b + 1):
            prepare(blk_i)
        prepared = max(prepared, b + 1)

    prepare_upto(order[0])
    yield
    nxt = scores(order[0])
    yield
    for pos, i in enumerate(order):
        cur = nxt
        if pos + 2 < n_blocks:
            prepare_upto(order[pos + 2])
        if pos + 1 < n_blocks:
            prepare_upto(order[pos + 1])
            nxt = scores(order[pos + 1])
            yield
        attend(i, cur)
        yield


def _moba_core(qk, vg, mc, ms, *, heads, dh, heads_per_step=2):
    b, s, _ = qk.shape
    blk = MOBA_BLOCK
    hp = heads_per_step
    assert s % blk == 0 and dh == LANES and heads % hp == 0
    n_blocks = s // blk
    groups = heads // hp
    spec = lambda off: pl.BlockSpec((1, s, hp * dh), lambda i, h: (i, 0, off + h))
    tab = pl.BlockSpec((1, s, dh), lambda i, h: (i, 0, 0))
    return pl.pallas_call(
        functools.partial(_moba_kernel, n_blocks=n_blocks, blk=blk, sub=MOBA_SCORE_ROWS,
                          topk=min(MOBA_TOPK, n_blocks),
                          exp_scale=dh ** -0.5 * math.log2(math.e), dh=dh),
        grid=(b, groups),
        in_specs=[spec(0), spec(groups), spec(0), spec(groups), tab, tab],
        out_specs=spec(0),
        out_shape=jax.ShapeDtypeStruct((b, s, heads * dh), BF16),
        scratch_shapes=[pltpu.VMEM((hp, s, dh), BF16), pltpu.VMEM((hp, s, dh), BF16),
                        pltpu.VMEM((hp, dh + BF16_SUBLANES, s), BF16),
                        pltpu.VMEM((hp, n_blocks, dh), F32)]
        + [pltpu.VMEM(((i + 1) * blk, blk), F32) for _ in range(hp) for i in range(n_blocks)],
        compiler_params=_params(("parallel", "parallel")),
        name="moba_core",
    )(qk, qk, vg, vg, mc, ms)


def _outproj_ln_kernel(u_ref, w_ref, x_ref, gate_ref, lng_ref, lnb_ref, o_ref, *, sub):
    n_sub = u_ref.shape[0] // sub

    def matmul(r):
        return jnp.dot(u_ref[r * sub:(r + 1) * sub, :], w_ref[...].astype(u_ref.dtype),
                       preferred_element_type=F32)

    y_next = matmul(0)
    for r in range(n_sub):
        rows = slice(r * sub, (r + 1) * sub)
        y = y_next
        if r + 1 < n_sub:
            y_next = matmul(r + 1)
        _layernorm_via(o_ref, rows, DEEPNORM_ALPHA * x_ref[rows, :] + gate_ref[0] * y,
                       lng_ref[...], lnb_ref[...])


def _outproj_ln(u2d, w_mat, x2d, gate, ln_g, ln_b, *, seq, tm, sub=256):
    m, kdim = u2d.shape
    d = w_mat.shape[1]
    assert seq % tm == 0 and tm % sub == 0
    tiles_per_seq = seq // tm
    row = pl.BlockSpec((1, d), lambda i: (0, 0))
    tile = pl.BlockSpec((tm, d), lambda i: (i, 0))
    return pl.pallas_call(
        functools.partial(_outproj_ln_kernel, sub=sub),
        grid=(m // tm,),
        in_specs=[
            pl.BlockSpec((tm, kdim), lambda i: (i, 0)),
            pl.BlockSpec((kdim, d), lambda i: (0, 0), pipeline_mode=pl.Buffered(1)),
            tile,
            pl.BlockSpec((1, 1, d), lambda i: (i // tiles_per_seq, 0, 0)),
            row, row,
        ],
        out_specs=tile,
        out_shape=jax.ShapeDtypeStruct((m, d), F32),
        compiler_params=_params(("parallel",)),
        name="outproj_ln",
    )(u2d, w_mat, x2d, gate, ln_g.reshape(1, d), ln_b.reshape(1, d))


def kernel(x, c, positions, ret_w_in, ret_w_out, moba_w_in, moba_w_out, w_ada, b_ada, ln_g, ln_b):
    b, s, d = x.shape
    assert DEPTH == 2 and w_ada.shape[0] == DEPTH
    ret_dk = d // RET_HEADS
    ret_dv = 2 * d // RET_HEADS
    moba_dh = d // MOBA_HEADS
    ret_qk = 2 * RET_HEADS * ret_dk
    moba_qk = 2 * MOBA_HEADS * moba_dh

    mod, (cos_r, sin_r, mc, ms), ret_w_qk, ret_w_o, moba_w_qk = _setup(
        c, w_ada, b_ada, positions, ret_w_in[0], ret_w_out[0], moba_w_in[0],
        ret_dk=ret_dk, ret_qk=ret_qk, moba_dh=moba_dh, moba_qk=moba_qk)
    shift, scale, gate = [[mod[l, :, k * d:(k + 1) * d].reshape(b, 1, d) for l in range(DEPTH)]
                          for k in range(3)]

    x2d = x.reshape(b * s, d)
    qk, h2d = _inproj_qk(x2d, shift[0], scale[0], ret_w_qk,
                         cos_r.reshape(b * s, LANES), sin_r.reshape(b * s, LANES),
                         seq=s, head_w=ret_dk, k_mult=ret_dk ** -0.5)
    vg = _inproj_plain(h2d, ret_w_in[0], col0=ret_qk)
    x1, h1 = _retention_outproj(qk.reshape(b, s, -1), vg.reshape(b, s, -1), ret_w_o, x, gate[0],
                                ln_g[0], ln_b[0], shift[1], scale[1],
                                heads=RET_HEADS, dk=ret_dk, dv=ret_dv)
    x2d, h2d = x1.reshape(b * s, d), h1.reshape(b * s, d)

    qk = _inproj_plain(h2d, moba_w_qk)
    vg = _inproj_plain(h2d, moba_w_in[0], col0=moba_qk)
    u = _moba_core(qk.reshape(b, s, -1), vg.reshape(b, s, -1), mc, ms,
                   heads=MOBA_HEADS, dh=moba_dh)
    x2d = _outproj_ln(u.reshape(b * s, -1), moba_w_out[0], x2d, gate[1],
                      ln_g[1], ln_b[1], seq=s, tm=512)
    return x2d.reshape(b, s, d)
```

```python
import functools
import math

import jax
import jax.numpy as jnp
import numpy as np
from jax import lax
from jax.experimental import pallas as pl
from jax.experimental.pallas import tpu as pltpu

F32 = jnp.float32
BF16 = jnp.bfloat16

DEPTH = 2
N_MIXERS = 2
DEEPNORM_ALPHA = (2.0 * DEPTH) ** 0.25
LN_EPS = 1e-5

RET_HEADS = 8
RET_CHUNK = 256
RET_THETA = 10000.0
RET_GN_EPS = 1e-5

MOBA_HEADS = 16
MOBA_BLOCK = 256
MOBA_TOPK = 3
MOBA_SCORE_ROWS = 256
ROPE_THETA = 500000.0
NEG = -1e30

LANES = 128
BF16_SUBLANES = 16
VMEM_LIMIT = 56 * 1024 * 1024


def _params(sem):
    return pltpu.CompilerParams(dimension_semantics=sem, vmem_limit_bytes=VMEM_LIMIT)


def _setup_kernel(c_ref, wada_ref, bada_ref, pos_ref, inv_r_ref, inv_m_ref,
                  rwin_ref, rwout_ref, mwin_ref,
                  mod_ref, cos_r, sin_r, mc, ms, rwin_o, rwout_o, mwin_o, *, half_m):
    acc = jnp.dot(c_ref[...].astype(BF16), wada_ref[...].astype(BF16),
                  preferred_element_type=F32)
    mod_ref[...] = acc + bada_ref[...]

    rwin_o[...] = rwin_ref[...].astype(rwin_o.dtype)
    rwout_o[...] = rwout_ref[...].astype(rwout_o.dtype)

    mid = LANES // 2
    for lo in range(0, mwin_ref.shape[1], LANES):
        x = mwin_ref[:, lo:lo + LANES]
        wlane = lax.broadcasted_iota(jnp.int32, x.shape, 1)
        up = pltpu.roll(x, mid - half_m, 1)
        dn = pltpu.roll(x, LANES - (mid - half_m), 1)
        y = jnp.where((wlane >= half_m) & (wlane < 2 * half_m), dn,
                      jnp.where((wlane >= mid) & (wlane < mid + half_m), up, x))
        mwin_o[:, lo:lo + LANES] = y.astype(mwin_o.dtype)

    pos = pos_ref[0].astype(F32)
    ang = pos * inv_r_ref[...]
    cos_r[0] = jnp.cos(ang)
    sin_r[0] = jnp.sin(ang)
    angm = pos * inv_m_ref[...]
    cm = jnp.cos(angm)
    sm = jnp.sin(angm)
    lane = lax.broadcasted_iota(jnp.int32, angm.shape, 1)
    first = lane < half_m
    second = (lane >= LANES // 2) & (lane < LANES // 2 + half_m)
    mc[0] = jnp.where(first | second, cm, 1.0)
    ms[0] = jnp.where(first, -sm, jnp.where(second, sm, 0.0))


def _setup(c, w_ada, b_ada, positions, ret_w_in, ret_w_out, moba_w_in, *,
           ret_dk, ret_qk, moba_dh, moba_qk, ts=1024):
    depth, d, n_mod = w_ada.shape
    b, s = positions.shape
    steps = b * s // ts
    seq_tiles = s // ts
    half_r = ret_dk // 2
    half_m = moba_dh // 8
    assert half_r == LANES and moba_dh == LANES and LANES % half_m == 0
    tn_mod = depth * n_mod // steps
    out_rows = ret_w_out.shape[0] // steps
    col_w = ret_qk // steps
    assert n_mod % tn_mod == 0 and tn_mod % LANES == 0 and out_rows % 8 == 0
    assert ret_qk == moba_qk == steps * col_w and col_w % LANES == 0
    mod_tiles = n_mod // tn_mod

    inv_r = RET_THETA ** (-jnp.arange(half_r, dtype=F32) * 2.0 / ret_dk)
    inv_m = ROPE_THETA ** (-jnp.arange(half_m, dtype=F32) * 2.0 / (2 * half_m))
    inv_m = jnp.tile(inv_m, LANES // half_m)

    row = pl.BlockSpec((1, LANES), lambda t: (0, 0))
    mod_idx = lambda t: (t // mod_tiles, 0, t % mod_tiles)
    tab_idx = lambda t: (t // seq_tiles, t % seq_tiles, 0)
    col_tile = pl.BlockSpec((d, col_w), lambda t: (0, t))
    row_tile = pl.BlockSpec((out_rows, ret_w_out.shape[1]), lambda t: (t, 0))
    tab = jax.ShapeDtypeStruct((b, s, LANES), F32)
    outs = pl.pallas_call(
        functools.partial(_setup_kernel, half_m=half_m),
        grid=(steps,),
        in_specs=[
            pl.BlockSpec((b, d), lambda t: (0, 0)),
            pl.BlockSpec((None, d, tn_mod), mod_idx),
            pl.BlockSpec((None, 1, tn_mod), mod_idx),
            pl.BlockSpec((1, ts, 1), tab_idx), row, row,
            col_tile, row_tile, col_tile,
        ],
        out_specs=[pl.BlockSpec((None, b, tn_mod), mod_idx)]
        + [pl.BlockSpec((1, ts, LANES), tab_idx)] * 4
        + [col_tile, row_tile, col_tile],
        out_shape=[jax.ShapeDtypeStruct((depth, b, n_mod), F32), tab, tab, tab, tab,
                   jax.ShapeDtypeStruct((d, ret_qk), BF16),
                   jax.ShapeDtypeStruct(ret_w_out.shape, BF16),
                   jax.ShapeDtypeStruct((d, moba_qk), BF16)],
        compiler_params=_params(("parallel",)),
        name="setup",
    )(c, w_ada, b_ada.reshape(depth, 1, n_mod), positions.reshape(b, s, 1),
      inv_r.reshape(1, LANES), inv_m.reshape(1, LANES), ret_w_in, ret_w_out, moba_w_in)
    return outs[0], outs[1:5], outs[5], outs[6], outs[7]


def _inproj_qk_kernel(x_ref, shift_ref, scale_ref, w_ref, cos_ref, sin_ref, o_ref, h_ref,
                      *, tn, head_w, k_mult):
    h = x_ref[...] * (1.0 + scale_ref[0]) + shift_ref[0]
    h_ref[...] = h.astype(h_ref.dtype)
    n = w_ref.shape[1]
    cos_t, sin_t = cos_ref[...], sin_ref[...]
    for c0 in range(0, n, tn):
        acc = jnp.dot(h_ref[...], w_ref[:, c0:c0 + tn], preferred_element_type=F32)
        is_q = c0 < n // 2
        c = cos_t if is_q else cos_t * k_mult
        s = sin_t if is_q else sin_t * k_mult
        for lo in range(0, tn, head_w):
            x1 = acc[:, lo:lo + LANES]
            x2 = acc[:, lo + LANES:lo + 2 * LANES]
            o_ref[:, c0 + lo:c0 + lo + LANES] = (x1 * c - x2 * s).astype(o_ref.dtype)
            o_ref[:, c0 + lo + LANES:c0 + lo + 2 * LANES] = (x2 * c + x1 * s).astype(o_ref.dtype)


def _inproj_qk(x2d, shift, scale, w_qk, cos_t, sin_t, *, seq, head_w, k_mult, tm=512, tn=1024):
    m, d = x2d.shape
    n = w_qk.shape[1]
    assert seq % tm == 0 and (n // 2) % tn == 0 and tn % head_w == 0 and head_w == 2 * LANES
    tiles_per_seq = seq // tm
    tab_spec = pl.BlockSpec((tm, LANES), lambda i: (i, 0))
    mod_spec = pl.BlockSpec((1, 1, d), lambda i: (i // tiles_per_seq, 0, 0))
    return pl.pallas_call(
        functools.partial(_inproj_qk_kernel, tn=tn, head_w=head_w, k_mult=k_mult),
        grid=(m // tm,),
        in_specs=[
            pl.BlockSpec((tm, d), lambda i: (i, 0)),
            mod_spec, mod_spec,
            pl.BlockSpec((d, n), lambda i: (0, 0), pipeline_mode=pl.Buffered(1)),
            tab_spec, tab_spec,
        ],
        out_specs=[pl.BlockSpec((tm, n), lambda i: (i, 0)),
                   pl.BlockSpec((tm, d), lambda i: (i, 0))],
        out_shape=[jax.ShapeDtypeStruct((m, n), BF16), jax.ShapeDtypeStruct((m, d), BF16)],
        compiler_params=_params(("parallel",)),
        name="inproj_qk",
    )(x2d, shift, scale, w_qk, cos_t, sin_t)


def _matmul_kernel(h_ref, w_ref, o_ref):
    o_ref[...] = jnp.dot(h_ref[...], w_ref[...].astype(h_ref.dtype),
                         preferred_element_type=F32).astype(o_ref.dtype)


def _inproj_plain(h2d, w_mat, *, col0=0, tm=2048, tn=1024):
    m, d = h2d.shape
    n = w_mat.shape[1] - col0
    assert m % tm == 0 and n % tn == 0 and col0 % tn == 0
    j0 = col0 // tn
    return pl.pallas_call(
        _matmul_kernel,
        grid=(m // tm, n // tn),
        in_specs=[pl.BlockSpec((tm, d), lambda i, j: (i, 0)),
                  pl.BlockSpec((d, tn), lambda i, j: (0, j0 + j))],
        out_specs=pl.BlockSpec((tm, tn), lambda i, j: (i, j)),
        out_shape=jax.ShapeDtypeStruct((m, n), BF16),
        compiler_params=_params(("parallel", "parallel")),
        name="inproj_plain",
    )(h2d, w_mat)


def _silu(g):
    half = 0.5 * g
    return half + half * jnp.tanh(half)


def _layernorm_via(o_ref, rows, z, lng, lnb):
    o_ref[rows] = z
    mu = jnp.mean(o_ref[rows], axis=-1, keepdims=True)
    var = jnp.mean(jnp.square(o_ref[rows] - mu), axis=-1, keepdims=True)
    xn = (o_ref[rows] - mu) * lax.rsqrt(var + LN_EPS) * lng + lnb
    o_ref[rows] = xn
    return xn


def _retention_outproj_kernel(q_ref, k_ref, v_ref, g_ref, dec_ref, qd_ref, kd_ref, cd_ref,
                              w_ref, x_ref, gate_ref, lng_ref, lnb_ref, nshift_ref, nscale_ref,
                              o_ref, h_ref, st_scr, *, heads, dk, dv):
    @pl.when(pl.program_id(1) == 0)
    def _():
        st_scr[...] = jnp.zeros_like(st_scr)

    def retain(hh):
        qc = q_ref[0, :, hh * dk:(hh + 1) * dk]
        kc = k_ref[0, :, hh * dk:(hh + 1) * dk]
        vc = v_ref[0, :, hh * dv:(hh + 1) * dv]
        scores = lax.dot_general(qc, kc, (((1,), (1,)), ((), ())),
                                 preferred_element_type=F32) * dec_ref[hh]
        inner = jnp.dot(scores.astype(BF16), vc, preferred_element_type=F32)
        kdec = (kc.astype(F32) * kd_ref[hh]).astype(BF16)
        update = lax.dot_general(kdec, vc, (((0,), (0,)), ((), ())), preferred_element_type=F32)
        state = st_scr[hh]
        cross = jnp.dot(qc, state.astype(BF16), preferred_element_type=F32) * qd_ref[hh]
        st_scr[hh] = state * cd_ref[hh] + update
        return inner + cross

    def gate_norm(hh, o):
        mu = jnp.mean(o, axis=-1, keepdims=True)
        oc = o - mu
        var = jnp.mean(oc * oc, axis=-1, keepdims=True)
        gn = oc * lax.rsqrt(var + RET_GN_EPS)
        g = g_ref[0, :, hh * dv:(hh + 1) * dv].astype(F32)
        return (gn * _silu(g)).astype(BF16)

    ahead = 2
    o = {hh: retain(hh) for hh in range(min(ahead, heads))}
    u = gate_norm(0, o.pop(0))
    y = None
    for hh in range(heads):
        if hh + ahead < heads:
            o[hh + ahead] = retain(hh + ahead)
        part = jnp.dot(u, w_ref[hh * dv:(hh + 1) * dv, :], preferred_element_type=F32)
        y = part if y is None else part + y
        if hh + 1 < heads:
            u = gate_norm(hh + 1, o.pop(hh + 1))

    xn = _layernorm_via(o_ref, 0, DEEPNORM_ALPHA * x_ref[0] + gate_ref[0] * y,
                        lng_ref[...], lnb_ref[...])
    h_ref[0] = (xn * (1.0 + nscale_ref[0]) + nshift_ref[0]).astype(h_ref.dtype)


def _retention_decays(heads, chunk):
    log_gamma = jnp.log(1.0 - 2.0 ** (-5.0 - jnp.arange(heads, dtype=F32)))
    n = jnp.arange(chunk, dtype=F32)
    diff = n[:, None] - n[None, :]
    inner = jnp.where(diff >= 0, jnp.exp(jnp.maximum(diff, 0.0) * log_gamma[:, None, None]), 0.0)
    q_decay = jnp.exp((n + 1.0) * log_gamma[:, None])
    k_decay = jnp.exp((chunk - 1.0 - n) * log_gamma[:, None])
    chunk_decay = jnp.exp(chunk * log_gamma)
    return (inner.astype(F32), q_decay.reshape(heads, chunk, 1), k_decay.reshape(heads, chunk, 1),
            chunk_decay.reshape(heads, 1, 1))


def _retention_outproj(qk, vg, w_out, x, gate, ln_g, ln_b, next_shift, next_scale,
                       *, heads, dk, dv):
    b, s, d = x.shape
    chunk = RET_CHUNK
    assert s % chunk == 0 and w_out.shape == (heads * dv, d)
    dec, qd, kd, cd = _retention_decays(heads, chunk)
    whole = lambda shape: pl.BlockSpec(shape, lambda i, c: (0,) * len(shape))
    rows = lambda width, blk: pl.BlockSpec((1, chunk, width), lambda i, c: (i, c, blk))
    per_batch = pl.BlockSpec((1, 1, d), lambda i, c: (i, 0, 0))
    return pl.pallas_call(
        functools.partial(_retention_outproj_kernel, heads=heads, dk=dk, dv=dv),
        grid=(b, s // chunk),
        in_specs=[
            rows(heads * dk, 0), rows(heads * dk, 1), rows(heads * dv, 0), rows(heads * dv, 1),
            whole((heads, chunk, chunk)), whole((heads, chunk, 1)), whole((heads, chunk, 1)),
            whole((heads, 1, 1)),
            pl.BlockSpec((heads * dv, d), lambda i, c: (0, 0), pipeline_mode=pl.Buffered(1)),
            rows(d, 0), per_batch, whole((1, d)), whole((1, d)), per_batch, per_batch,
        ],
        out_specs=[rows(d, 0), rows(d, 0)],
        out_shape=[jax.ShapeDtypeStruct((b, s, d), F32), jax.ShapeDtypeStruct((b, s, d), BF16)],
        scratch_shapes=[pltpu.VMEM((heads, dk, dv), F32)],
        compiler_params=_params(("parallel", "arbitrary")),
        name="retention_outproj",
    )(qk, qk, vg, vg, dec, qd, kd, cd, w_out, x, gate, ln_g.reshape(1, d), ln_b.reshape(1, d),
      next_shift, next_scale)


def _moba_kernel(q_ref, k_ref, v_ref, g_ref, mc_ref, ms_ref, o_ref,
                 qr_scrs, kr_scrs, vt_scrs, km_scrs, *s_scrs,
                 n_blocks, blk, sub, topk, exp_scale, dh):
    n_heads = q_ref.shape[2] // dh
    programs = [_moba_head_program(hh, q_ref, k_ref, v_ref, g_ref, mc_ref, ms_ref, o_ref,
                                   qr_scrs.at[hh], kr_scrs.at[hh], vt_scrs.at[hh], km_scrs.at[hh],
                                   s_scrs[hh * n_blocks:(hh + 1) * n_blocks],
                                   n_blocks=n_blocks, blk=blk, sub=sub, topk=topk,
                                   exp_scale=exp_scale, dh=dh)
                for hh in range(n_heads)]
    while programs:
        programs = [p for p in programs if next(p, "done") != "done"]


def _moba_head_program(hh, q_ref, k_ref, v_ref, g_ref, mc_ref, ms_ref, o_ref,
                       qr_scr, kr_scr, vt_scr, km_scr, s_scrs, *,
                       n_blocks, blk, sub, topk, exp_scale, dh):
    lanes = slice(hh * dh, (hh + 1) * dh)
    vt_scr[dh:, :] = jnp.ones((vt_scr.shape[0] - dh, vt_scr.shape[1]), BF16)
    km_scr[...] = jnp.zeros_like(km_scr)

    def prepare(b):
        rows = slice(b * blk, (b + 1) * blk)
        mc = mc_ref[0, rows, :]
        ms = ms_ref[0, rows, :]

        def rotary(x_ref):
            x = x_ref[0, rows, lanes].astype(F32)
            return x * mc + pltpu.roll(x, LANES // 2, 1) * ms

        qr_scr[rows, :] = (rotary(q_ref) * exp_scale).astype(BF16)
        kr = rotary(k_ref)
        kr_scr[rows, :] = kr.astype(BF16)
        km_scr[b:b + 1, :] = jnp.mean(kr, axis=0, keepdims=True)
        vt_scr[:dh, rows] = v_ref[0, rows, lanes].T

    nt = (((1,), (1,)), ((), ()))
    key_id = lax.broadcasted_iota(jnp.int32, (sub, blk), 0)
    qry_id = lax.broadcasted_iota(jnp.int32, (sub, blk), 1)
    blk_id = lax.broadcasted_iota(jnp.int32, (n_blocks, blk), 0)

    def scores(i):
        cols = slice(i * blk, (i + 1) * blk)
        s_scr = s_scrs[i]
        qi = qr_scr[cols, :]
        ranked = i > topk
        if ranked:
            gi = lax.dot_general(km_scr[...].astype(BF16), qi, nt,
                                 preferred_element_type=F32)
            past = blk_id < i
        m = None
        bias = []
        for j in range(i + 1):
            mj = None
            for part in range(blk // sub):
                keys = slice(j * blk + part * sub, j * blk + (part + 1) * sub)
                sj = lax.dot_general(kr_scr[keys, :], qi, nt,
                                     preferred_element_type=F32)
                if j == i:
                    sj = jnp.where(key_id + part * sub <= qry_id, sj, NEG)
                s_scr[keys, :] = sj
                mp = jnp.max(sj, axis=0, keepdims=True)
                mj = mp if mj is None else jnp.maximum(mj, mp)
            if ranked and j < i:
                gj = gi[j:j + 1, :]
                beats = past & ((gi > gj) | ((gi == gj) & (blk_id < j)))
                cnt = jnp.sum(jnp.where(beats, 1.0, 0.0), axis=0, keepdims=True)
                bias.append(jnp.where(cnt < topk, 0.0, NEG))
                mj = mj + bias[-1]
            m = mj if m is None else jnp.maximum(m, mj)
        if not ranked:
            return [m]
        return [m - bj for bj in bias] + [m]

    def attend(i, subs):
        cols = slice(i * blk, (i + 1) * blk)
        ot = None
        for j in range(i + 1):
            keys = slice(j * blk, (j + 1) * blk)
            e = jnp.exp2(s_scrs[i][keys, :] - subs[min(j, len(subs) - 1)]).astype(BF16)
            pj = jnp.dot(vt_scr[:, keys], e, preferred_element_type=F32)
            ot = pj if ot is None else pj + ot
        o = (ot[:dh, :] * (1.0 / ot[dh:dh + 1, :])).T
        g = g_ref[0, cols, lanes].astype(F32)
        o_ref[0, cols, lanes] = (o * _silu(g)).astype(o_ref.dtype)

    order = list(range(1, n_blocks, 2)) + list(range((n_blocks - 1) // 2 * 2, -1, -2))
    prepared = 0

    def prepare_upto(b):
        nonlocal prepared
        for blk_i in range(prepared, b + 1):
            prepare(blk_i)
        prepared = max(prepared, b + 1)

    prepare_upto(order[0])
    yield
    nxt = scores(order[0])
    yield
    for pos, i in enumerate(order):
        cur = nxt
        if pos + 2 < n_blocks:
            prepare_upto(order[pos + 2])
        if pos + 1 < n_blocks:
            prepare_upto(order[pos + 1])
            nxt = scores(order[pos + 1])
            yield
        attend(i, cur)
        yield


def _moba_core(qk, vg, mc, ms, *, heads, dh, heads_per_step=2):
    b, s, _ = qk.shape
    blk = MOBA_BLOCK
    hp = heads_per_step
    assert s % blk == 0 and dh == LANES and heads % hp == 0
    n_blocks = s // blk
    groups = heads // hp
    spec = lambda off: pl.BlockSpec((1, s, hp * dh), lambda i, h: (i, 0, off + h))
    tab = pl.BlockSpec((1, s, dh), lambda i, h: (i, 0, 0))
    return pl.pallas_call(
        functools.partial(_moba_kernel, n_blocks=n_blocks, blk=blk, sub=MOBA_SCORE_ROWS,
                          topk=min(MOBA_TOPK, n_blocks),
                          exp_scale=dh ** -0.5 * math.log2(math.e), dh=dh),
        grid=(b, groups),
        in_specs=[spec(0), spec(groups), spec(0), spec(groups), tab, tab],
        out_specs=spec(0),
        out_shape=jax.ShapeDtypeStruct((b, s, heads * dh), BF16),
        scratch_shapes=[pltpu.VMEM((hp, s, dh), BF16), pltpu.VMEM((hp, s, dh), BF16),
                        pltpu.VMEM((hp, dh + BF16_SUBLANES, s), BF16),
                        pltpu.VMEM((hp, n_blocks, dh), F32)]
        + [pltpu.VMEM(((i + 1) * blk, blk), F32) for _ in range(hp) for i in range(n_blocks)],
        compiler_params=_params(("parallel", "parallel")),
        name="moba_core",
    )(qk, qk, vg, vg, mc, ms)


def _outproj_ln_kernel(u_ref, w_ref, x_ref, gate_ref, lng_ref, lnb_ref, o_ref, *, sub):
    n_sub = u_ref.shape[0] // sub

    def matmul(r):
        return jnp.dot(u_ref[r * sub:(r + 1) * sub, :], w_ref[...].astype(u_ref.dtype),
                       preferred_element_type=F32)

    y_next = matmul(0)
    for r in range(n_sub):
        rows = slice(r * sub, (r + 1) * sub)
        y = y_next
        if r + 1 < n_sub:
            y_next = matmul(r + 1)
        _layernorm_via(o_ref, rows, DEEPNORM_ALPHA * x_ref[rows, :] + gate_ref[0] * y,
                       lng_ref[...], lnb_ref[...])


def _outproj_ln(u2d, w_mat, x2d, gate, ln_g, ln_b, *, seq, tm, sub=256):
    m, kdim = u2d.shape
    d = w_mat.shape[1]
    assert seq % tm == 0 and tm % sub == 0
    tiles_per_seq = seq // tm
    row = pl.BlockSpec((1, d), lambda i: (0, 0))
    tile = pl.BlockSpec((tm, d), lambda i: (i, 0))
    return pl.pallas_call(
        functools.partial(_outproj_ln_kernel, sub=sub),
        grid=(m // tm,),
        in_specs=[
            pl.BlockSpec((tm, kdim), lambda i: (i, 0)),
            pl.BlockSpec((kdim, d), lambda i: (0, 0), pipeline_mode=pl.Buffered(1)),
            tile,
            pl.BlockSpec((1, 1, d), lambda i: (i // tiles_per_seq, 0, 0)),
            row, row,
        ],
        out_specs=tile,
        out_shape=jax.ShapeDtypeStruct((m, d), F32),
        compiler_params=_params(("parallel",)),
        name="outproj_ln",
    )(u2d, w_mat, x2d, gate, ln_g.reshape(1, d), ln_b.reshape(1, d))


def kernel(x, c, positions, ret_w_in, ret_w_out, moba_w_in, moba_w_out, w_ada, b_ada, ln_g, ln_b):
    b, s, d = x.shape
    assert DEPTH == 2 and w_ada.shape[0] == DEPTH
    ret_dk = d // RET_HEADS
    ret_dv = 2 * d // RET_HEADS
    moba_dh = d // MOBA_HEADS
    ret_qk = 2 * RET_HEADS * ret_dk
    moba_qk = 2 * MOBA_HEADS * moba_dh

    mod, (cos_r, sin_r, mc, ms), ret_w_qk, ret_w_o, moba_w_qk = _setup(
        c, w_ada, b_ada, positions, ret_w_in[0], ret_w_out[0], moba_w_in[0],
        ret_dk=ret_dk, ret_qk=ret_qk, moba_dh=moba_dh, moba_qk=moba_qk)
    shift, scale, gate = [[mod[l, :, k * d:(k + 1) * d].reshape(b, 1, d) for l in range(DEPTH)]
                          for k in range(3)]

    x2d = x.reshape(b * s, d)
    qk, h2d = _inproj_qk(x2d, shift[0], scale[0], ret_w_qk,
                         cos_r.reshape(b * s, LANES), sin_r.reshape(b * s, LANES),
                         seq=s, head_w=ret_dk, k_mult=ret_dk ** -0.5)
    vg = _inproj_plain(h2d, ret_w_in[0], col0=ret_qk)
    x1, h1 = _retention_outproj(qk.reshape(b, s, -1), vg.reshape(b, s, -1), ret_w_o, x, gate[0],
                                ln_g[0], ln_b[0], shift[1], scale[1],
                                heads=RET_HEADS, dk=ret_dk, dv=ret_dv)
    x2d, h2d = x1.reshape(b * s, d), h1.reshape(b * s, d)

    qk = _inproj_plain(h2d, moba_w_qk)
    vg = _inproj_plain(h2d, moba_w_in[0], col0=moba_qk)
    u = _moba_core(qk.reshape(b, s, -1), vg.reshape(b, s, -1), mc, ms,
                   heads=MOBA_HEADS, dh=moba_dh)
    x2d = _outproj_ln(u.reshape(b * s, -1), moba_w_out[0], x2d, gate[1],
                      ln_g[1], ln_b[1], seq=s, tm=512)
    return x2d.reshape(b, s, d)
```

```python
import functools
import math

import jax
import jax.numpy as jnp
import numpy as np
from jax import lax
from jax.experimental import pallas as pl
from jax.experimental.pallas import tpu as pltpu

F32 = jnp.float32
BF16 = jnp.bfloat16

DEPTH = 2
N_MIXERS = 2
DEEPNORM_ALPHA = (2.0 * DEPTH) ** 0.25
LN_EPS = 1e-5

RET_HEADS = 8
RET_CHUNK = 256
RET_THETA = 10000.0
RET_GN_EPS = 1e-5

MOBA_HEADS = 16
MOBA_BLOCK = 256
MOBA_TOPK = 3
MOBA_SCORE_ROWS = 256
ROPE_THETA = 500000.0
NEG = -1e30

LANES = 128
BF16_SUBLANES = 16
VMEM_LIMIT = 56 * 1024 * 1024


def _params(sem):
    return pltpu.CompilerParams(dimension_semantics=sem, vmem_limit_bytes=VMEM_LIMIT)


def _setup_kernel(c_ref, wada_ref, bada_ref, pos_ref, posrow_ref, inv_r_ref, inv_m_ref,
                  rwin_ref, rwout_ref, mwin_ref,
                  mod_ref, cos_r, sin_r, mct, mst, rwin_o, rwout_o, mwin_o, *, half_m):
    acc = jnp.dot(c_ref[...].astype(BF16), wada_ref[...].astype(BF16),
                  preferred_element_type=F32)
    mod_ref[...] = acc + bada_ref[...]

    rwin_o[...] = rwin_ref[...].astype(rwin_o.dtype)
    rwout_o[...] = rwout_ref[...].astype(rwout_o.dtype)

    mid = LANES // 2
    for lo in range(0, mwin_ref.shape[1], LANES):
        x = mwin_ref[:, lo:lo + LANES]
        wlane = lax.broadcasted_iota(jnp.int32, x.shape, 1)
        up = pltpu.roll(x, mid - half_m, 1)
        dn = pltpu.roll(x, LANES - (mid - half_m), 1)
        y = jnp.where((wlane >= half_m) & (wlane < 2 * half_m), dn,
                      jnp.where((wlane >= mid) & (wlane < mid + half_m), up, x))
        mwin_o[:, lo:lo + LANES] = y.astype(mwin_o.dtype)

    pos = pos_ref[0].astype(F32)
    ang = pos * inv_r_ref[...]
    cos_r[0] = jnp.cos(ang)
    sin_r[0] = jnp.sin(ang)
    angm = inv_m_ref[...] * posrow_ref[0].astype(F32)
    cm = jnp.cos(angm)
    sm = jnp.sin(angm)
    gap = LANES // 2 - half_m
    ones = jnp.ones((gap, angm.shape[1]), F32)
    zeros = jnp.zeros((gap, angm.shape[1]), F32)
    mct[0] = jnp.concatenate([cm, ones, cm, ones], axis=0)
    mst[0] = jnp.concatenate([-sm, zeros, sm, zeros], axis=0)


def _setup(c, w_ada, b_ada, positions, ret_w_in, ret_w_out, moba_w_in, *,
           ret_dk, ret_qk, moba_dh, moba_qk, ts=1024):
    depth, d, n_mod = w_ada.shape
    b, s = positions.shape
    steps = b * s // ts
    seq_tiles = s // ts
    half_r = ret_dk // 2
    half_m = moba_dh // 8
    assert half_r == LANES and moba_dh == LANES and LANES % half_m == 0
    tn_mod = depth * n_mod // steps
    out_rows = ret_w_out.shape[0] // steps
    col_w = ret_qk // steps
    assert n_mod % tn_mod == 0 and tn_mod % LANES == 0 and out_rows % 8 == 0
    assert ret_qk == moba_qk == steps * col_w and col_w % LANES == 0
    mod_tiles = n_mod // tn_mod

    inv_r = RET_THETA ** (-jnp.arange(half_r, dtype=F32) * 2.0 / ret_dk)
    inv_m = ROPE_THETA ** (-jnp.arange(half_m, dtype=F32) * 2.0 / (2 * half_m))

    mod_idx = lambda t: (t // mod_tiles, 0, t % mod_tiles)
    tab_idx = lambda t: (t // seq_tiles, t % seq_tiles, 0)
    tabt_idx = lambda t: (t // seq_tiles, 0, t % seq_tiles)
    col_tile = pl.BlockSpec((d, col_w), lambda t: (0, t))
    row_tile = pl.BlockSpec((out_rows, ret_w_out.shape[1]), lambda t: (t, 0))
    tab = jax.ShapeDtypeStruct((b, s, LANES), F32)
    tabt = jax.ShapeDtypeStruct((b, LANES, s), F32)
    outs = pl.pallas_call(
        functools.partial(_setup_kernel, half_m=half_m),
        grid=(steps,),
        in_specs=[
            pl.BlockSpec((b, d), lambda t: (0, 0)),
            pl.BlockSpec((None, d, tn_mod), mod_idx),
            pl.BlockSpec((None, 1, tn_mod), mod_idx),
            pl.BlockSpec((1, ts, 1), tab_idx),
            pl.BlockSpec((1, 1, ts), tabt_idx),
            pl.BlockSpec((1, LANES), lambda t: (0, 0)),
            pl.BlockSpec((half_m, 1), lambda t: (0, 0)),
            col_tile, row_tile, col_tile,
        ],
        out_specs=[pl.BlockSpec((None, b, tn_mod), mod_idx)]
        + [pl.BlockSpec((1, ts, LANES), tab_idx)] * 2
        + [pl.BlockSpec((1, LANES, ts), tabt_idx)] * 2
        + [col_tile, row_tile, col_tile],
        out_shape=[jax.ShapeDtypeStruct((depth, b, n_mod), F32), tab, tab, tabt, tabt,
                   jax.ShapeDtypeStruct((d, ret_qk), BF16),
                   jax.ShapeDtypeStruct(ret_w_out.shape, BF16),
                   jax.ShapeDtypeStruct((d, moba_qk), BF16)],
        compiler_params=_params(("parallel",)),
        name="setup",
    )(c, w_ada, b_ada.reshape(depth, 1, n_mod), positions.reshape(b, s, 1),
      positions.reshape(b, 1, s), inv_r.reshape(1, LANES), inv_m.reshape(half_m, 1),
      ret_w_in, ret_w_out, moba_w_in)
    return outs[0], outs[1:5], outs[5], outs[6], outs[7]


def _inproj_qk_kernel(x_ref, shift_ref, scale_ref, w_ref, cos_ref, sin_ref, o_ref, h_ref,
                      *, tn, head_w, k_mult):
    h = x_ref[...] * (1.0 + scale_ref[0]) + shift_ref[0]
    h_ref[...] = h.astype(h_ref.dtype)
    n = w_ref.shape[1]
    cos_t, sin_t = cos_ref[...], sin_ref[...]
    for c0 in range(0, n, tn):
        acc = jnp.dot(h_ref[...], w_ref[:, c0:c0 + tn], preferred_element_type=F32)
        is_q = c0 < n // 2
        c = cos_t if is_q else cos_t * k_mult
        s = sin_t if is_q else sin_t * k_mult
        for lo in range(0, tn, head_w):
            x1 = acc[:, lo:lo + LANES]
            x2 = acc[:, lo + LANES:lo + 2 * LANES]
            o_ref[:, c0 + lo:c0 + lo + LANES] = (x1 * c - x2 * s).astype(o_ref.dtype)
            o_ref[:, c0 + lo + LANES:c0 + lo + 2 * LANES] = (x2 * c + x1 * s).astype(o_ref.dtype)


def _inproj_qk(x2d, shift, scale, w_qk, cos_t, sin_t, *, seq, head_w, k_mult, tm=512, tn=1024):
    m, d = x2d.shape
    n = w_qk.shape[1]
    assert seq % tm == 0 and (n // 2) % tn == 0 and tn % head_w == 0 and head_w == 2 * LANES
    tiles_per_seq = seq // tm
    tab_spec = pl.BlockSpec((tm, LANES), lambda i: (i, 0))
    mod_spec = pl.BlockSpec((1, 1, d), lambda i: (i // tiles_per_seq, 0, 0))
    return pl.pallas_call(
        functools.partial(_inproj_qk_kernel, tn=tn, head_w=head_w, k_mult=k_mult),
        grid=(m // tm,),
        in_specs=[
            pl.BlockSpec((tm, d), lambda i: (i, 0)),
            mod_spec, mod_spec,
            pl.BlockSpec((d, n), lambda i: (0, 0), pipeline_mode=pl.Buffered(1)),
            tab_spec, tab_spec,
        ],
        out_specs=[pl.BlockSpec((tm, n), lambda i: (i, 0)),
                   pl.BlockSpec((tm, d), lambda i: (i, 0))],
        out_shape=[jax.ShapeDtypeStruct((m, n), BF16), jax.ShapeDtypeStruct((m, d), BF16)],
        compiler_params=_params(("parallel",)),
        name="inproj_qk",
    )(x2d, shift, scale, w_qk, cos_t, sin_t)


def _matmul_kernel(h_ref, w_ref, o_ref):
    o_ref[...] = jnp.dot(h_ref[...], w_ref[...].astype(h_ref.dtype),
                         preferred_element_type=F32).astype(o_ref.dtype)


def _inproj_plain(h2d, w_mat, *, col0=0, tm=2048, tn=1024):
    m, d = h2d.shape
    n = w_mat.shape[1] - col0
    assert m % tm == 0 and n % tn == 0 and col0 % tn == 0
    j0 = col0 // tn
    return pl.pallas_call(
        _matmul_kernel,
        grid=(m // tm, n // tn),
        in_specs=[pl.BlockSpec((tm, d), lambda i, j: (i, 0)),
                  pl.BlockSpec((d, tn), lambda i, j: (0, j0 + j))],
        out_specs=pl.BlockSpec((tm, tn), lambda i, j: (i, j)),
        out_shape=jax.ShapeDtypeStruct((m, n), BF16),
        compiler_params=_params(("parallel", "parallel")),
        name="inproj_plain",
    )(h2d, w_mat)


def _silu(g):
    half = 0.5 * g
    return half + half * jnp.tanh(half)


def _layernorm_via(o_ref, rows, z, lng, lnb):
    o_ref[rows] = z
    mu = jnp.mean(o_ref[rows], axis=-1, keepdims=True)
    var = jnp.mean(jnp.square(o_ref[rows] - mu), axis=-1, keepdims=True)
    xn = (o_ref[rows] - mu) * lax.rsqrt(var + LN_EPS) * lng + lnb
    o_ref[rows] = xn
    return xn


def _retention_outproj_kernel(q_ref, k_ref, v_ref, g_ref, dec_ref, qd_ref, kd_ref, cd_ref,
                              w_ref, x_ref, gate_ref, lng_ref, lnb_ref, nshift_ref, nscale_ref,
                              o_ref, h_ref, st_scr, *, heads, dk, dv):
    @pl.when(pl.program_id(1) == 0)
    def _():
        st_scr[...] = jnp.zeros_like(st_scr)

    def retain(hh):
        qc = q_ref[0, :, hh * dk:(hh + 1) * dk]
        kc = k_ref[0, :, hh * dk:(hh + 1) * dk]
        vc = v_ref[0, :, hh * dv:(hh + 1) * dv]
        scores = lax.dot_general(qc, kc, (((1,), (1,)), ((), ())),
                                 preferred_element_type=F32) * dec_ref[hh]
        inner = jnp.dot(scores.astype(BF16), vc, preferred_element_type=F32)
        kdec = (kc.astype(F32) * kd_ref[hh]).astype(BF16)
        update = lax.dot_general(kdec, vc, (((0,), (0,)), ((), ())), preferred_element_type=F32)
        state = st_scr[hh]
        cross = jnp.dot(qc, state.astype(BF16), preferred_element_type=F32) * qd_ref[hh]
        st_scr[hh] = state * cd_ref[hh] + update
        return inner + cross

    def gate_norm(hh, o):
        mu = jnp.mean(o, axis=-1, keepdims=True)
        oc = o - mu
        var = jnp.mean(oc * oc, axis=-1, keepdims=True)
        gn = oc * lax.rsqrt(var + RET_GN_EPS)
        g = g_ref[0, :, hh * dv:(hh + 1) * dv].astype(F32)
        return (gn * _silu(g)).astype(BF16)

    ahead = 2
    o = {hh: retain(hh) for hh in range(min(ahead, heads))}
    u = gate_norm(0, o.pop(0))
    y = None
    for hh in range(heads):
        if hh + ahead < heads:
            o[hh + ahead] = retain(hh + ahead)
        part = jnp.dot(u, w_ref[hh * dv:(hh + 1) * dv, :], preferred_element_type=F32)
        y = part if y is None else part + y
        if hh + 1 < heads:
            u = gate_norm(hh + 1, o.pop(hh + 1))

    xn = _layernorm_via(o_ref, 0, DEEPNORM_ALPHA * x_ref[0] + gate_ref[0] * y,
                        lng_ref[...], lnb_ref[...])
    h_ref[0] = (xn * (1.0 + nscale_ref[0]) + nshift_ref[0]).astype(h_ref.dtype)


def _retention_decays(heads, chunk):
    log_gamma = jnp.log(1.0 - 2.0 ** (-5.0 - jnp.arange(heads, dtype=F32)))
    n = jnp.arange(chunk, dtype=F32)
    diff = n[:, None] - n[None, :]
    inner = jnp.where(diff >= 0, jnp.exp(jnp.maximum(diff, 0.0) * log_gamma[:, None, None]), 0.0)
    q_decay = jnp.exp((n + 1.0) * log_gamma[:, None])
    k_decay = jnp.exp((chunk - 1.0 - n) * log_gamma[:, None])
    chunk_decay = jnp.exp(chunk * log_gamma)
    return (inner.astype(F32), q_decay.reshape(heads, chunk, 1), k_decay.reshape(heads, chunk, 1),
            chunk_decay.reshape(heads, 1, 1))


def _retention_outproj(qk, vg, w_out, x, gate, ln_g, ln_b, next_shift, next_scale,
                       *, heads, dk, dv):
    b, s, d = x.shape
    chunk = RET_CHUNK
    assert s % chunk == 0 and w_out.shape == (heads * dv, d)
    dec, qd, kd, cd = _retention_decays(heads, chunk)
    whole = lambda shape: pl.BlockSpec(shape, lambda i, c: (0,) * len(shape))
    rows = lambda width, blk: pl.BlockSpec((1, chunk, width), lambda i, c: (i, c, blk))
    per_batch = pl.BlockSpec((1, 1, d), lambda i, c: (i, 0, 0))
    return pl.pallas_call(
        functools.partial(_retention_outproj_kernel, heads=heads, dk=dk, dv=dv),
        grid=(b, s // chunk),
        in_specs=[
            rows(heads * dk, 0), rows(heads * dk, 1), rows(heads * dv, 0), rows(heads * dv, 1),
            whole((heads, chunk, chunk)), whole((heads, chunk, 1)), whole((heads, chunk, 1)),
            whole((heads, 1, 1)),
            pl.BlockSpec((heads * dv, d), lambda i, c: (0, 0), pipeline_mode=pl.Buffered(1)),
            rows(d, 0), per_batch, whole((1, d)), whole((1, d)), per_batch, per_batch,
        ],
        out_specs=[rows(d, 0), rows(d, 0)],
        out_shape=[jax.ShapeDtypeStruct((b, s, d), F32), jax.ShapeDtypeStruct((b, s, d), BF16)],
        scratch_shapes=[pltpu.VMEM((heads, dk, dv), F32)],
        compiler_params=_params(("parallel", "arbitrary")),
        name="retention_outproj",
    )(qk, qk, vg, vg, dec, qd, kd, cd, w_out, x, gate, ln_g.reshape(1, d), ln_b.reshape(1, d),
      next_shift, next_scale)


def _moba_kernel(q_ref, k_ref, v_ref, g_ref, mct_ref, mst_ref, o_ref,
                 mc_scr, ms_scr, qr_scrs, kr_scrs, vt_scrs, km_scrs, *s_scrs,
                 n_blocks, blk, sub, topk, exp_scale, dh):
    @pl.when(pl.program_id(1) == 0)
    def _():
        mc_scr[...] = mct_ref[0].T
        ms_scr[...] = mst_ref[0].T

    n_heads = q_ref.shape[2] // dh
    programs = [_moba_head_program(hh, q_ref, k_ref, v_ref, g_ref, mc_scr, ms_scr, o_ref,
                                   qr_scrs.at[hh], kr_scrs.at[hh], vt_scrs.at[hh], km_scrs.at[hh],
                                   s_scrs[hh * n_blocks:(hh + 1) * n_blocks],
                                   n_blocks=n_blocks, blk=blk, sub=sub, topk=topk,
                                   exp_scale=exp_scale, dh=dh)
                for hh in range(n_heads)]
    while programs:
        programs = [p for p in programs if next(p, "done") != "done"]


def _moba_head_program(hh, q_ref, k_ref, v_ref, g_ref, mc_scr, ms_scr, o_ref,
                       qr_scr, kr_scr, vt_scr, km_scr, s_scrs, *,
                       n_blocks, blk, sub, topk, exp_scale, dh):
    lanes = slice(hh * dh, (hh + 1) * dh)
    vt_scr[dh:, :] = jnp.ones((vt_scr.shape[0] - dh, vt_scr.shape[1]), BF16)
    km_scr[...] = jnp.zeros_like(km_scr)

    def prepare(b):
        rows = slice(b * blk, (b + 1) * blk)
        mc = mc_scr[rows, :]
        ms = ms_scr[rows, :]

        def rotary(x_ref):
            x = x_ref[0, rows, lanes].astype(F32)
            return x * mc + pltpu.roll(x, LANES // 2, 1) * ms

        qr_scr[rows, :] = (rotary(q_ref) * exp_scale).astype(BF16)
        kr = rotary(k_ref)
        kr_scr[rows, :] = kr.astype(BF16)
        km_scr[b:b + 1, :] = jnp.mean(kr, axis=0, keepdims=True)
        vt_scr[:dh, rows] = v_ref[0, rows, lanes].T

    nt = (((1,), (1,)), ((), ()))
    key_id = lax.broadcasted_iota(jnp.int32, (sub, blk), 0)
    qry_id = lax.broadcasted_iota(jnp.int32, (sub, blk), 1)
    blk_id = lax.broadcasted_iota(jnp.int32, (n_blocks, blk), 0)

    def scores(i):
        cols = slice(i * blk, (i + 1) * blk)
        s_scr = s_scrs[i]
        qi = qr_scr[cols, :]
        ranked = i > topk
        if ranked:
            gi = lax.dot_general(km_scr[...].astype(BF16), qi, nt,
                                 preferred_element_type=F32)
            past = blk_id < i
        m = None
        bias = []
        for j in range(i + 1):
            mj = None
            for part in range(blk // sub):
                keys = slice(j * blk + part * sub, j * blk + (part + 1) * sub)
                sj = lax.dot_general(kr_scr[keys, :], qi, nt,
                                     preferred_element_type=F32)
                if j == i:
                    sj = jnp.where(key_id + part * sub <= qry_id, sj, NEG)
                s_scr[keys, :] = sj
                mp = jnp.max(sj, axis=0, keepdims=True)
                mj = mp if mj is None else jnp.maximum(mj, mp)
            if ranked and j < i:
                gj = gi[j:j + 1, :]
                beats = past & ((gi > gj) | ((gi == gj) & (blk_id < j)))
                cnt = jnp.sum(jnp.where(beats, 1.0, 0.0), axis=0, keepdims=True)
                bias.append(jnp.where(cnt < topk, 0.0, NEG))
                mj = mj + bias[-1]
            m = mj if m is None else jnp.maximum(m, mj)
        if not ranked:
            return [m]
        return [m - bj for bj in bias] + [m]

    def attend(i, subs):
        cols = slice(i * blk, (i + 1) * blk)
        ot = None
        for j in range(i + 1):
            keys = slice(j * blk, (j + 1) * blk)
            e = jnp.exp2(s_scrs[i][keys, :] - subs[min(j, len(subs) - 1)]).astype(BF16)
            pj = jnp.dot(vt_scr[:, keys], e, preferred_element_type=F32)
            ot = pj if ot is None else pj + ot
        o = (ot[:dh, :] * (1.0 / ot[dh:dh + 1, :])).T
        g = g_ref[0, cols, lanes].astype(F32)
        o_ref[0, cols, lanes] = (o * _silu(g)).astype(o_ref.dtype)

    order = list(range(1, n_blocks, 2)) + list(range((n_blocks - 1) // 2 * 2, -1, -2))
    prepared = 0

    def prepare_upto(b):
        nonlocal prepared
        for blk_i in range(prepared, b + 1):
            prepare(blk_i)
        prepared = max(prepared, b + 1)

    prepare_upto(order[0])
    yield
    nxt = scores(order[0])
    yield
    for pos, i in enumerate(order):
        cur = nxt
        if pos + 2 < n_blocks:
            prepare_upto(order[pos + 2])
        if pos + 1 < n_blocks:
            prepare_upto(order[pos + 1])
            nxt = scores(order[pos + 1])
            yield
        attend(i, cur)
        yield


def _moba_core(qk, vg, mct, mst, *, heads, dh, heads_per_step=2):
    b, s, _ = qk.shape
    blk = MOBA_BLOCK
    hp = heads_per_step
    assert s % blk == 0 and dh == LANES and heads % hp == 0
    n_blocks = s // blk
    groups = heads // hp
    spec = lambda off: pl.BlockSpec((1, s, hp * dh), lambda i, h: (i, 0, off + h))
    tab = pl.BlockSpec((1, dh, s), lambda i, h: (i, 0, 0))
    return pl.pallas_call(
        functools.partial(_moba_kernel, n_blocks=n_blocks, blk=blk, sub=MOBA_SCORE_ROWS,
                          topk=min(MOBA_TOPK, n_blocks),
                          exp_scale=dh ** -0.5 * math.log2(math.e), dh=dh),
        grid=(b, groups),
        in_specs=[spec(0), spec(groups), spec(0), spec(groups), tab, tab],
        out_specs=spec(0),
        out_shape=jax.ShapeDtypeStruct((b, s, heads * dh), BF16),
        scratch_shapes=[pltpu.VMEM((s, dh), F32), pltpu.VMEM((s, dh), F32),
                        pltpu.VMEM((hp, s, dh), BF16), pltpu.VMEM((hp, s, dh), BF16),
                        pltpu.VMEM((hp, dh + BF16_SUBLANES, s), BF16),
                        pltpu.VMEM((hp, n_blocks, dh), F32)]
        + [pltpu.VMEM(((i + 1) * blk, blk), F32) for _ in range(hp) for i in range(n_blocks)],
        compiler_params=_params(("parallel", "arbitrary")),
        name="moba_core",
    )(qk, qk, vg, vg, mct, mst)


def _outproj_ln_kernel(u_ref, w_ref, x_ref, gate_ref, lng_ref, lnb_ref, o_ref, *, sub):
    n_sub = u_ref.shape[0] // sub

    def matmul(r):
        return jnp.dot(u_ref[r * sub:(r + 1) * sub, :], w_ref[...].astype(u_ref.dtype),
                       preferred_element_type=F32)

    y_next = matmul(0)
    for r in range(n_sub):
        rows = slice(r * sub, (r + 1) * sub)
        y = y_next
        if r + 1 < n_sub:
            y_next = matmul(r + 1)
        _layernorm_via(o_ref, rows, DEEPNORM_ALPHA * x_ref[rows, :] + gate_ref[0] * y,
                       lng_ref[...], lnb_ref[...])


def _outproj_ln(u2d, w_mat, x2d, gate, ln_g, ln_b, *, seq, tm, sub=256):
    m, kdim = u2d.shape
    d = w_mat.shape[1]
    assert seq % tm == 0 and tm % sub == 0
    tiles_per_seq = seq // tm
    row = pl.BlockSpec((1, d), lambda i: (0, 0))
    tile = pl.BlockSpec((tm, d), lambda i: (i, 0))
    return pl.pallas_call(
        functools.partial(_outproj_ln_kernel, sub=sub),
        grid=(m // tm,),
        in_specs=[
            pl.BlockSpec((tm, kdim), lambda i: (i, 0)),
            pl.BlockSpec((kdim, d), lambda i: (0, 0), pipeline_mode=pl.Buffered(1)),
            tile,
            pl.BlockSpec((1, 1, d), lambda i: (i // tiles_per_seq, 0, 0)),
            row, row,
        ],
        out_specs=tile,
        out_shape=jax.ShapeDtypeStruct((m, d), F32),
        compiler_params=_params(("parallel",)),
        name="outproj_ln",
    )(u2d, w_mat, x2d, gate, ln_g.reshape(1, d), ln_b.reshape(1, d))


def kernel(x, c, positions, ret_w_in, ret_w_out, moba_w_in, moba_w_out, w_ada, b_ada, ln_g, ln_b):
    b, s, d = x.shape
    assert DEPTH == 2 and w_ada.shape[0] == DEPTH
    ret_dk = d // RET_HEADS
    ret_dv = 2 * d // RET_HEADS
    moba_dh = d // MOBA_HEADS
    ret_qk = 2 * RET_HEADS * ret_dk
    moba_qk = 2 * MOBA_HEADS * moba_dh

    mod, (cos_r, sin_r, mct, mst), ret_w_qk, ret_w_o, moba_w_qk = _setup(
        c, w_ada, b_ada, positions, ret_w_in[0], ret_w_out[0], moba_w_in[0],
        ret_dk=ret_dk, ret_qk=ret_qk, moba_dh=moba_dh, moba_qk=moba_qk)
    shift, scale, gate = [[mod[l, :, k * d:(k + 1) * d].reshape(b, 1, d) for l in range(DEPTH)]
                          for k in range(3)]

    x2d = x.reshape(b * s, d)
    qk, h2d = _inproj_qk(x2d, shift[0], scale[0], ret_w_qk,
                         cos_r.reshape(b * s, LANES), sin_r.reshape(b * s, LANES),
                         seq=s, head_w=ret_dk, k_mult=ret_dk ** -0.5)
    vg = _inproj_plain(h2d, ret_w_in[0], col0=ret_qk)
    x1, h1 = _retention_outproj(qk.reshape(b, s, -1), vg.reshape(b, s, -1), ret_w_o, x, gate[0],
                                ln_g[0], ln_b[0], shift[1], scale[1],
                                heads=RET_HEADS, dk=ret_dk, dv=ret_dv)
    x2d, h2d = x1.reshape(b * s, d), h1.reshape(b * s, d)

    qk = _inproj_plain(h2d, moba_w_qk)
    vg = _inproj_plain(h2d, moba_w_in[0], col0=moba_qk)
    u = _moba_core(qk.reshape(b, s, -1), vg.reshape(b, s, -1), mct, mst,
                   heads=MOBA_HEADS, dh=moba_dh)
    x2d = _outproj_ln(u.reshape(b * s, -1), moba_w_out[0], x2d, gate[1],
                      ln_g[1], ln_b[1], seq=s, tm=512)
    return x2d.reshape(b, s, d)
```

```python
import functools
import math

import jax
import jax.numpy as jnp
from jax import lax
from jax.experimental import pallas as pl
from jax.experimental.pallas import tpu as pltpu

F32 = jnp.float32
BF16 = jnp.bfloat16

DEPTH = 2
DEEPNORM_ALPHA = (2.0 * DEPTH) ** 0.25
LN_EPS = 1e-5

RET_HEADS = 8
RET_CHUNK = 256
RET_THETA = 10000.0
RET_GN_EPS = 1e-5

MOBA_HEADS = 16
MOBA_BLOCK = 256
MOBA_TOPK = 3
MOBA_SCORE_ROWS = 256
ROPE_THETA = 500000.0
NEG = -1e30

LANES = 128
BF16_SUBLANES = 16
VMEM_LIMIT = 56 * 1024 * 1024


def _params(sem):
    return pltpu.CompilerParams(dimension_semantics=sem, vmem_limit_bytes=VMEM_LIMIT)


def _setup_kernel(c_ref, wada_ref, bada_ref, pos_ref, posrow_ref, inv_r_ref, inv_m_ref,
                  rwin_ref, rwout_ref, mwin_ref,
                  mod_ref, cos_r, sin_r, mct, mst, rwin_o, rwout_o, mwin_o, *, half_m):
    acc = jnp.dot(c_ref[...].astype(BF16), wada_ref[...].astype(BF16),
                  preferred_element_type=F32)
    mod_ref[...] = acc + bada_ref[...]

    rwin_o[...] = rwin_ref[...].astype(rwin_o.dtype)
    rwout_o[...] = rwout_ref[...].astype(rwout_o.dtype)

    mid = LANES // 2
    for lo in range(0, mwin_ref.shape[1], LANES):
        x = mwin_ref[:, lo:lo + LANES]
        wlane = lax.broadcasted_iota(jnp.int32, x.shape, 1)
        up = pltpu.roll(x, mid - half_m, 1)
        dn = pltpu.roll(x, LANES - (mid - half_m), 1)
        y = jnp.where((wlane >= half_m) & (wlane < 2 * half_m), dn,
                      jnp.where((wlane >= mid) & (wlane < mid + half_m), up, x))
        mwin_o[:, lo:lo + LANES] = y.astype(mwin_o.dtype)

    pos = pos_ref[0].astype(F32)
    ang = pos * inv_r_ref[...]
    cos_r[0] = jnp.cos(ang)
    sin_r[0] = jnp.sin(ang)
    angm = inv_m_ref[...] * posrow_ref[0].astype(F32)
    cm = jnp.cos(angm)
    sm = jnp.sin(angm)
    gap = LANES // 2 - half_m
    ones = jnp.ones((gap, angm.shape[1]), F32)
    zeros = jnp.zeros((gap, angm.shape[1]), F32)
    mct[0] = jnp.concatenate([cm, ones, cm, ones], axis=0)
    mst[0] = jnp.concatenate([-sm, zeros, sm, zeros], axis=0)


def _setup(c, w_ada, b_ada, positions, ret_w_in, ret_w_out, moba_w_in, *,
           ret_dk, ret_qk, moba_dh, moba_qk, ts=1024):
    depth, d, n_mod = w_ada.shape
    b, s = positions.shape
    steps = b * s // ts
    seq_tiles = s // ts
    half_r = ret_dk // 2
    half_m = moba_dh // 8
    assert half_r == LANES and moba_dh == LANES and LANES % half_m == 0
    tn_mod = depth * n_mod // steps
    out_rows = ret_w_out.shape[0] // steps
    col_w = ret_qk // steps
    assert n_mod % tn_mod == 0 and tn_mod % LANES == 0 and out_rows % 8 == 0
    assert ret_qk == moba_qk == steps * col_w and col_w % LANES == 0
    mod_tiles = n_mod // tn_mod

    inv_r = RET_THETA ** (-jnp.arange(half_r, dtype=F32) * 2.0 / ret_dk)
    inv_m = ROPE_THETA ** (-jnp.arange(half_m, dtype=F32) * 2.0 / (2 * half_m))

    mod_idx = lambda t: (t // mod_tiles, 0, t % mod_tiles)
    tab_idx = lambda t: (t // seq_tiles, t % seq_tiles, 0)
    tabt_idx = lambda t: (t // seq_tiles, 0, t % seq_tiles)
    col_tile = pl.BlockSpec((d, col_w), lambda t: (0, t))
    row_tile = pl.BlockSpec((out_rows, ret_w_out.shape[1]), lambda t: (t, 0))
    tab = jax.ShapeDtypeStruct((b, s, LANES), F32)
    tabt = jax.ShapeDtypeStruct((b, LANES, s), F32)
    outs = pl.pallas_call(
        functools.partial(_setup_kernel, half_m=half_m),
        grid=(steps,),
        in_specs=[
            pl.BlockSpec((b, d), lambda t: (0, 0)),
            pl.BlockSpec((None, d, tn_mod), mod_idx),
            pl.BlockSpec((None, 1, tn_mod), mod_idx),
            pl.BlockSpec((1, ts, 1), tab_idx),
            pl.BlockSpec((1, 1, ts), tabt_idx),
            pl.BlockSpec((1, LANES), lambda t: (0, 0)),
            pl.BlockSpec((half_m, 1), lambda t: (0, 0)),
            col_tile, row_tile, col_tile,
        ],
        out_specs=[pl.BlockSpec((None, b, tn_mod), mod_idx)]
        + [pl.BlockSpec((1, ts, LANES), tab_idx)] * 2
        + [pl.BlockSpec((1, LANES, ts), tabt_idx)] * 2
        + [col_tile, row_tile, col_tile],
        out_shape=[jax.ShapeDtypeStruct((depth, b, n_mod), F32), tab, tab, tabt, tabt,
                   jax.ShapeDtypeStruct((d, ret_qk), BF16),
                   jax.ShapeDtypeStruct(ret_w_out.shape, BF16),
                   jax.ShapeDtypeStruct((d, moba_qk), BF16)],
        compiler_params=_params(("parallel",)),
        name="setup",
    )(c, w_ada, b_ada.reshape(depth, 1, n_mod), positions.reshape(b, s, 1),
      positions.reshape(b, 1, s), inv_r.reshape(1, LANES), inv_m.reshape(half_m, 1),
      ret_w_in, ret_w_out, moba_w_in)
    return outs[0], outs[1:5], outs[5], outs[6], outs[7]


def _inproj_qk_kernel(x_ref, shift_ref, scale_ref, w_ref, cos_ref, sin_ref, o_ref, h_ref,
                      *, tn, head_w, k_mult):
    h = x_ref[...] * (1.0 + scale_ref[0]) + shift_ref[0]
    h_ref[...] = h.astype(h_ref.dtype)
    n = w_ref.shape[1]
    cos_t, sin_t = cos_ref[...], sin_ref[...]
    for c0 in range(0, n, tn):
        acc = jnp.dot(h_ref[...], w_ref[:, c0:c0 + tn], preferred_element_type=F32)
        is_q = c0 < n // 2
        c = cos_t if is_q else cos_t * k_mult
        s = sin_t if is_q else sin_t * k_mult
        for lo in range(0, tn, head_w):
            x1 = acc[:, lo:lo + LANES]
            x2 = acc[:, lo + LANES:lo + 2 * LANES]
            o_ref[:, c0 + lo:c0 + lo + LANES] = (x1 * c - x2 * s).astype(o_ref.dtype)
            o_ref[:, c0 + lo + LANES:c0 + lo + 2 * LANES] = (x2 * c + x1 * s).astype(o_ref.dtype)


def _inproj_qk(x2d, shift, scale, w_qk, cos_t, sin_t, *, seq, head_w, k_mult, tm=512, tn=1024):
    m, d = x2d.shape
    n = w_qk.shape[1]
    assert seq % tm == 0 and (n // 2) % tn == 0 and tn % head_w == 0 and head_w == 2 * LANES
    tiles_per_seq = seq // tm
    tab_spec = pl.BlockSpec((tm, LANES), lambda i: (i, 0))
    mod_spec = pl.BlockSpec((1, 1, d), lambda i: (i // tiles_per_seq, 0, 0))
    return pl.pallas_call(
        functools.partial(_inproj_qk_kernel, tn=tn, head_w=head_w, k_mult=k_mult),
        grid=(m // tm,),
        in_specs=[
            pl.BlockSpec((tm, d), lambda i: (i, 0)),
            mod_spec, mod_spec,
            pl.BlockSpec((d, n), lambda i: (0, 0), pipeline_mode=pl.Buffered(1)),
            tab_spec, tab_spec,
        ],
        out_specs=[pl.BlockSpec((tm, n), lambda i: (i, 0)),
                   pl.BlockSpec((tm, d), lambda i: (i, 0))],
        out_shape=[jax.ShapeDtypeStruct((m, n), BF16), jax.ShapeDtypeStruct((m, d), BF16)],
        compiler_params=_params(("parallel",)),
        name="inproj_qk",
    )(x2d, shift, scale, w_qk, cos_t, sin_t)


def _matmul_kernel(h_ref, w_ref, o_ref):
    o_ref[...] = jnp.dot(h_ref[...], w_ref[...].astype(h_ref.dtype),
                         preferred_element_type=F32).astype(o_ref.dtype)


def _inproj_plain(h2d, w_mat, *, col0=0, tm=2048, tn=1024):
    m, d = h2d.shape
    n = w_mat.shape[1] - col0
    assert m % tm == 0 and n % tn == 0 and col0 % tn == 0
    j0 = col0 // tn
    return pl.pallas_call(
        _matmul_kernel,
        grid=(m // tm, n // tn),
        in_specs=[pl.BlockSpec((tm, d), lambda i, j: (i, 0)),
                  pl.BlockSpec((d, tn), lambda i, j: (0, j0 + j))],
        out_specs=pl.BlockSpec((tm, tn), lambda i, j: (i, j)),
        out_shape=jax.ShapeDtypeStruct((m, n), BF16),
        compiler_params=_params(("parallel", "parallel")),
        name="inproj_plain",
    )(h2d, w_mat)


def _silu(g):
    half = 0.5 * g
    return half + half * jnp.tanh(half)


def _layernorm_via(o_ref, rows, z, lng, lnb):
    o_ref[rows] = z
    mu = jnp.mean(o_ref[rows], axis=-1, keepdims=True)
    var = jnp.mean(jnp.square(o_ref[rows] - mu), axis=-1, keepdims=True)
    xn = (o_ref[rows] - mu) * lax.rsqrt(var + LN_EPS) * lng + lnb
    o_ref[rows] = xn
    return xn


def _retention_outproj_kernel(q_ref, k_ref, v_ref, g_ref, dec_ref, qd_ref, kd_ref, cd_ref,
                              w_ref, x_ref, gate_ref, lng_ref, lnb_ref, nshift_ref, nscale_ref,
                              o_ref, h_ref, st_scr, *, heads, dk, dv):
    @pl.when(pl.program_id(1) == 0)
    def _():
        st_scr[...] = jnp.zeros_like(st_scr)

    def retain(hh):
        qc = q_ref[0, :, hh * dk:(hh + 1) * dk]
        kc = k_ref[0, :, hh * dk:(hh + 1) * dk]
        vc = v_ref[0, :, hh * dv:(hh + 1) * dv]
        scores = lax.dot_general(qc, kc, (((1,), (1,)), ((), ())),
                                 preferred_element_type=F32) * dec_ref[hh]
        inner = jnp.dot(scores.astype(BF16), vc, preferred_element_type=F32)
        kdec = (kc.astype(F32) * kd_ref[hh]).astype(BF16)
        update = lax.dot_general(kdec, vc, (((0,), (0,)), ((), ())), preferred_element_type=F32)
        state = st_scr[hh]
        cross = jnp.dot(qc, state.astype(BF16), preferred_element_type=F32) * qd_ref[hh]
        st_scr[hh] = state * cd_ref[hh] + update
        return inner + cross

    def gate_norm(hh, o):
        mu = jnp.mean(o, axis=-1, keepdims=True)
        oc = o - mu
        var = jnp.mean(oc * oc, axis=-1, keepdims=True)
        gn = oc * lax.rsqrt(var + RET_GN_EPS)
        g = g_ref[0, :, hh * dv:(hh + 1) * dv].astype(F32)
        return (gn * _silu(g)).astype(BF16)

    ahead = 2
    o = {hh: retain(hh) for hh in range(min(ahead, heads))}
    u = gate_norm(0, o.pop(0))
    y = None
    for hh in range(heads):
        if hh + ahead < heads:
            o[hh + ahead] = retain(hh + ahead)
        part = jnp.dot(u, w_ref[hh * dv:(hh + 1) * dv, :], preferred_element_type=F32)
        y = part if y is None else part + y
        if hh + 1 < heads:
            u = gate_norm(hh + 1, o.pop(hh + 1))

    xn = _layernorm_via(o_ref, 0, DEEPNORM_ALPHA * x_ref[0] + gate_ref[0] * y,
                        lng_ref[...], lnb_ref[...])
    h_ref[0] = (xn * (1.0 + nscale_ref[0]) + nshift_ref[0]).astype(h_ref.dtype)


def _retention_decays(heads, chunk):
    log_gamma = jnp.log(1.0 - 2.0 ** (-5.0 - jnp.arange(heads, dtype=F32)))
    n = jnp.arange(chunk, dtype=F32)
    diff = n[:, None] - n[None, :]
    inner = jnp.where(diff >= 0, jnp.exp(jnp.maximum(diff, 0.0) * log_gamma[:, None, None]), 0.0)
    q_decay = jnp.exp((n + 1.0) * log_gamma[:, None])
    k_decay = jnp.exp((chunk - 1.0 - n) * log_gamma[:, None])
    chunk_decay = jnp.exp(chunk * log_gamma)
    return (inner.astype(F32), q_decay.reshape(heads, chunk, 1), k_decay.reshape(heads, chunk, 1),
            chunk_decay.reshape(heads, 1, 1))


def _retention_outproj(qk, vg, w_out, x, gate, ln_g, ln_b, next_shift, next_scale,
                       *, heads, dk, dv):
    b, s, d = x.shape
    chunk = RET_CHUNK
    assert s % chunk == 0 and w_out.shape == (heads * dv, d)
    dec, qd, kd, cd = _retention_decays(heads, chunk)
    whole = lambda shape: pl.BlockSpec(shape, lambda i, c: (0,) * len(shape))
    rows = lambda width, blk: pl.BlockSpec((1, chunk, width), lambda i, c: (i, c, blk))
    per_batch = pl.BlockSpec((1, 1, d), lambda i, c: (i, 0, 0))
    return pl.pallas_call(
        functools.partial(_retention_outproj_kernel, heads=heads, dk=dk, dv=dv),
        grid=(b, s // chunk),
        in_specs=[
            rows(heads * dk, 0), rows(heads * dk, 1), rows(heads * dv, 0), rows(heads * dv, 1),
            whole((heads, chunk, chunk)), whole((heads, chunk, 1)), whole((heads, chunk, 1)),
            whole((heads, 1, 1)),
            pl.BlockSpec((heads * dv, d), lambda i, c: (0, 0), pipeline_mode=pl.Buffered(1)),
            rows(d, 0), per_batch, whole((1, d)), whole((1, d)), per_batch, per_batch,
        ],
        out_specs=[rows(d, 0), rows(d, 0)],
        out_shape=[jax.ShapeDtypeStruct((b, s, d), F32), jax.ShapeDtypeStruct((b, s, d), BF16)],
        scratch_shapes=[pltpu.VMEM((heads, dk, dv), F32)],
        compiler_params=_params(("parallel", "arbitrary")),
        name="retention_outproj",
    )(qk, qk, vg, vg, dec, qd, kd, cd, w_out, x, gate, ln_g.reshape(1, d), ln_b.reshape(1, d),
      next_shift, next_scale)


def _moba_kernel(q_ref, k_ref, v_ref, g_ref, mct_ref, mst_ref, o_ref,
                 mc_scr, ms_scr, qr_scrs, kr_scrs, vt_scrs, km_scrs, *s_scrs,
                 n_blocks, blk, sub, topk, exp_scale, dh):
    @pl.when(pl.program_id(1) == 0)
    def _():
        mc_scr[...] = mct_ref[0].T
        ms_scr[...] = mst_ref[0].T

    n_heads = q_ref.shape[2] // dh
    programs = [_moba_head_program(hh, q_ref, k_ref, v_ref, g_ref, mc_scr, ms_scr, o_ref,
                                   qr_scrs.at[hh], kr_scrs.at[hh], vt_scrs.at[hh], km_scrs.at[hh],
                                   s_scrs[hh * n_blocks:(hh + 1) * n_blocks],
                                   n_blocks=n_blocks, blk=blk, sub=sub, topk=topk,
                                   exp_scale=exp_scale, dh=dh)
                for hh in range(n_heads)]
    while programs:
        programs = [p for p in programs if next(p, "done") != "done"]


def _moba_head_program(hh, q_ref, k_ref, v_ref, g_ref, mc_scr, ms_scr, o_ref,
                       qr_scr, kr_scr, vt_scr, km_scr, s_scrs, *,
                       n_blocks, blk, sub, topk, exp_scale, dh):
    lanes = slice(hh * dh, (hh + 1) * dh)
    vt_scr[dh:, :] = jnp.ones((vt_scr.shape[0] - dh, vt_scr.shape[1]), BF16)
    km_scr[...] = jnp.zeros_like(km_scr)

    def prepare(b):
        rows = slice(b * blk, (b + 1) * blk)
        mc = mc_scr[rows, :]
        ms = ms_scr[rows, :]

        def rotary(x_ref):
            x = x_ref[0, rows, lanes].astype(F32)
            return x * mc + pltpu.roll(x, LANES // 2, 1) * ms

        qr_scr[rows, :] = (rotary(q_ref) * exp_scale).astype(BF16)
        kr = rotary(k_ref)
        kr_scr[rows, :] = kr.astype(BF16)
        km_scr[b:b + 1, :] = jnp.mean(kr, axis=0, keepdims=True)
        vt_scr[:dh, rows] = v_ref[0, rows, lanes].T

    nt = (((1,), (1,)), ((), ()))
    key_id = lax.broadcasted_iota(jnp.int32, (sub, blk), 0)
    qry_id = lax.broadcasted_iota(jnp.int32, (sub, blk), 1)
    blk_id = lax.broadcasted_iota(jnp.int32, (n_blocks, blk), 0)

    def scores(i):
        cols = slice(i * blk, (i + 1) * blk)
        s_scr = s_scrs[i]
        qi = qr_scr[cols, :]
        ranked = i > topk
        if ranked:
            gi = lax.dot_general(km_scr[...].astype(BF16), qi, nt,
                                 preferred_element_type=F32)
            past = blk_id < i
        m = None
        bias = []
        for j in range(i + 1):
            mj = None
            for part in range(blk // sub):
                keys = slice(j * blk + part * sub, j * blk + (part + 1) * sub)
                sj = lax.dot_general(kr_scr[keys, :], qi, nt,
                                     preferred_element_type=F32)
                if j == i:
                    sj = jnp.where(key_id + part * sub <= qry_id, sj, NEG)
                s_scr[keys, :] = sj
                mp = jnp.max(sj, axis=0, keepdims=True)
                mj = mp if mj is None else jnp.maximum(mj, mp)
            if ranked and j < i:
                gj = gi[j:j + 1, :]
                beats = past & ((gi > gj) | ((gi == gj) & (blk_id < j)))
                cnt = jnp.sum(jnp.where(beats, 1.0, 0.0), axis=0, keepdims=True)
                bias.append(jnp.where(cnt < topk, 0.0, NEG))
                mj = mj + bias[-1]
            m = mj if m is None else jnp.maximum(m, mj)
        if not ranked:
            return [m]
        return [m - bj for bj in bias] + [m]

    def attend(i, subs):
        cols = slice(i * blk, (i + 1) * blk)
        ot = None
        for j in range(i + 1):
            keys = slice(j * blk, (j + 1) * blk)
            e = jnp.exp2(s_scrs[i][keys, :] - subs[min(j, len(subs) - 1)]).astype(BF16)
            pj = jnp.dot(vt_scr[:, keys], e, preferred_element_type=F32)
            ot = pj if ot is None else pj + ot
        o = (ot[:dh, :] * (1.0 / ot[dh:dh + 1, :])).T
        g = g_ref[0, cols, lanes].astype(F32)
        o_ref[0, cols, lanes] = (o * _silu(g)).astype(o_ref.dtype)

    order = list(range(1, n_blocks, 2)) + list(range((n_blocks - 1) // 2 * 2, -1, -2))
    prepared = 0

    def prepare_upto(b):
        nonlocal prepared
        for blk_i in range(prepared, b + 1):
            prepare(blk_i)
        prepared = max(prepared, b + 1)

    prepare_upto(order[0])
    yield
    nxt = scores(order[0])
    yield
    for pos, i in enumerate(order):
        cur = nxt
        if pos + 2 < n_blocks:
            prepare_upto(order[pos + 2])
        if pos + 1 < n_blocks:
            prepare_upto(order[pos + 1])
            nxt = scores(order[pos + 1])
            yield
        attend(i, cur)
        yield


def _moba_core(qk, vg, mct, mst, *, heads, dh, heads_per_step=2):
    b, s, _ = qk.shape
    blk = MOBA_BLOCK
    hp = heads_per_step
    assert s % blk == 0 and dh == LANES and heads % hp == 0
    n_blocks = s // blk
    groups = heads // hp
    spec = lambda off: pl.BlockSpec((1, s, hp * dh), lambda i, h: (i, 0, off + h))
    tab = pl.BlockSpec((1, dh, s), lambda i, h: (i, 0, 0))
    return pl.pallas_call(
        functools.partial(_moba_kernel, n_blocks=n_blocks, blk=blk, sub=MOBA_SCORE_ROWS,
                          topk=min(MOBA_TOPK, n_blocks),
                          exp_scale=dh ** -0.5 * math.log2(math.e), dh=dh),
        grid=(b, groups),
        in_specs=[spec(0), spec(groups), spec(0), spec(groups), tab, tab],
        out_specs=spec(0),
        out_shape=jax.ShapeDtypeStruct((b, s, heads * dh), BF16),
        scratch_shapes=[pltpu.VMEM((s, dh), F32), pltpu.VMEM((s, dh), F32),
                        pltpu.VMEM((hp, s, dh), BF16), pltpu.VMEM((hp, s, dh), BF16),
                        pltpu.VMEM((hp, dh + BF16_SUBLANES, s), BF16),
                        pltpu.VMEM((hp, n_blocks, dh), F32)]
        + [pltpu.VMEM(((i + 1) * blk, blk), F32) for _ in range(hp) for i in range(n_blocks)],
        compiler_params=_params(("parallel", "arbitrary")),
        name="moba_core",
    )(qk, qk, vg, vg, mct, mst)


def _outproj_ln_kernel(u_ref, w_ref, x_ref, gate_ref, lng_ref, lnb_ref, o_ref, *, sub):
    n_sub = u_ref.shape[0] // sub

    def matmul(r):
        return jnp.dot(u_ref[r * sub:(r + 1) * sub, :], w_ref[...].astype(u_ref.dtype),
                       preferred_element_type=F32)

    y_next = matmul(0)
    for r in range(n_sub):
        rows = slice(r * sub, (r + 1) * sub)
        y = y_next
        if r + 1 < n_sub:
            y_next = matmul(r + 1)
        _layernorm_via(o_ref, rows, DEEPNORM_ALPHA * x_ref[rows, :] + gate_ref[0] * y,
                       lng_ref[...], lnb_ref[...])


def _outproj_ln(u2d, w_mat, x2d, gate, ln_g, ln_b, *, seq, tm, sub=256):
    m, kdim = u2d.shape
    d = w_mat.shape[1]
    assert seq % tm == 0 and tm % sub == 0
    tiles_per_seq = seq // tm
    row = pl.BlockSpec((1, d), lambda i: (0, 0))
    tile = pl.BlockSpec((tm, d), lambda i: (i, 0))
    return pl.pallas_call(
        functools.partial(_outproj_ln_kernel, sub=sub),
        grid=(m // tm,),
        in_specs=[
            pl.BlockSpec((tm, kdim), lambda i: (i, 0)),
            pl.BlockSpec((kdim, d), lambda i: (0, 0), pipeline_mode=pl.Buffered(1)),
            tile,
            pl.BlockSpec((1, 1, d), lambda i: (i // tiles_per_seq, 0, 0)),
            row, row,
        ],
        out_specs=tile,
        out_shape=jax.ShapeDtypeStruct((m, d), F32),
        compiler_params=_params(("parallel",)),
        name="outproj_ln",
    )(u2d, w_mat, x2d, gate, ln_g.reshape(1, d), ln_b.reshape(1, d))


def kernel(x, c, positions, ret_w_in, ret_w_out, moba_w_in, moba_w_out, w_ada, b_ada, ln_g, ln_b):
    b, s, d = x.shape
    assert DEPTH == 2 and w_ada.shape[0] == DEPTH
    ret_dk = d // RET_HEADS
    ret_dv = 2 * d // RET_HEADS
    moba_dh = d // MOBA_HEADS
    ret_qk = 2 * RET_HEADS * ret_dk
    moba_qk = 2 * MOBA_HEADS * moba_dh

    mod, (cos_r, sin_r, mct, mst), ret_w_qk, ret_w_o, moba_w_qk = _setup(
        c, w_ada, b_ada, positions, ret_w_in[0], ret_w_out[0], moba_w_in[0],
        ret_dk=ret_dk, ret_qk=ret_qk, moba_dh=moba_dh, moba_qk=moba_qk)
    shift, scale, gate = [[mod[l, :, k * d:(k + 1) * d].reshape(b, 1, d) for l in range(DEPTH)]
                          for k in range(3)]

    x2d = x.reshape(b * s, d)
    qk, h2d = _inproj_qk(x2d, shift[0], scale[0], ret_w_qk,
                         cos_r.reshape(b * s, LANES), sin_r.reshape(b * s, LANES),
                         seq=s, head_w=ret_dk, k_mult=ret_dk ** -0.5)
    vg = _inproj_plain(h2d, ret_w_in[0], col0=ret_qk)
    x1, h1 = _retention_outproj(qk.reshape(b, s, -1), vg.reshape(b, s, -1), ret_w_o, x, gate[0],
                                ln_g[0], ln_b[0], shift[1], scale[1],
                                heads=RET_HEADS, dk=ret_dk, dv=ret_dv)
    x2d, h2d = x1.reshape(b * s, d), h1.reshape(b * s, d)

    qk = _inproj_plain(h2d, moba_w_qk)
    vg = _inproj_plain(h2d, moba_w_in[0], col0=moba_qk)
    u = _moba_core(qk.reshape(b, s, -1), vg.reshape(b, s, -1), mct, mst,
                   heads=MOBA_HEADS, dh=moba_dh)
    x2d = _outproj_ln(u.reshape(b * s, -1), moba_w_out[0], x2d, gate[1],
                      ln_g[1], ln_b[1], seq=s, tm=512)
    return x2d.reshape(b, s, d)
```

```python
import functools
import math

import jax
import jax.numpy as jnp
from jax import lax
from jax.experimental import pallas as pl
from jax.experimental.pallas import tpu as pltpu

F32 = jnp.float32
BF16 = jnp.bfloat16

DEPTH = 2
DEEPNORM_ALPHA = (2.0 * DEPTH) ** 0.25
LN_EPS = 1e-5

RET_HEADS = 8
RET_CHUNK = 256
RET_THETA = 10000.0
RET_GN_EPS = 1e-5

MOBA_HEADS = 16
MOBA_BLOCK = 256
MOBA_TOPK = 3
MOBA_SCORE_ROWS = 256
ROPE_THETA = 500000.0
NEG = -1e30

LANES = 128
BF16_SUBLANES = 16
VMEM_LIMIT = 56 * 1024 * 1024


def _params(sem):
    return pltpu.CompilerParams(dimension_semantics=sem, vmem_limit_bytes=VMEM_LIMIT)


def _setup_kernel(c_ref, wada_ref, bada_ref, pos_ref, posrow_ref, inv_r_ref, inv_m_ref,
                  rwin_ref, rwout_ref, mwin_ref,
                  mod_ref, cos_r, sin_r, mct, mst, rwin_o, rwout_o, mwin_o, *, half_m):
    acc = jnp.dot(c_ref[...].astype(BF16), wada_ref[...].astype(BF16),
                  preferred_element_type=F32)
    mod_ref[...] = acc + bada_ref[...]

    rwin_o[...] = rwin_ref[...].astype(rwin_o.dtype)
    rwout_o[...] = rwout_ref[...].astype(rwout_o.dtype)

    mid = LANES // 2
    for lo in range(0, mwin_ref.shape[1], LANES):
        x = mwin_ref[:, lo:lo + LANES]
        wlane = lax.broadcasted_iota(jnp.int32, x.shape, 1)
        up = pltpu.roll(x, mid - half_m, 1)
        dn = pltpu.roll(x, LANES - (mid - half_m), 1)
        y = jnp.where((wlane >= half_m) & (wlane < 2 * half_m), dn,
                      jnp.where((wlane >= mid) & (wlane < mid + half_m), up, x))
        mwin_o[:, lo:lo + LANES] = y.astype(mwin_o.dtype)

    pos = pos_ref[0].astype(F32)
    ang = pos * inv_r_ref[...]
    cos_r[0] = jnp.cos(ang)
    sin_r[0] = jnp.sin(ang)
    angm = inv_m_ref[...] * posrow_ref[0].astype(F32)
    cm = jnp.cos(angm)
    sm = jnp.sin(angm)
    gap = LANES // 2 - half_m
    ones = jnp.ones((gap, angm.shape[1]), F32)
    zeros = jnp.zeros((gap, angm.shape[1]), F32)
    mct[0] = jnp.concatenate([cm, ones, cm, ones], axis=0)
    mst[0] = jnp.concatenate([-sm, zeros, sm, zeros], axis=0)


def _setup(c, w_ada, b_ada, positions, ret_w_in, ret_w_out, moba_w_in, *,
           ret_dk, ret_qk, moba_dh, moba_qk, ts=1024):
    depth, d, n_mod = w_ada.shape
    b, s = positions.shape
    steps = b * s // ts
    seq_tiles = s // ts
    half_r = ret_dk // 2
    half_m = moba_dh // 8
    assert half_r == LANES and moba_dh == LANES and LANES % half_m == 0
    tn_mod = depth * n_mod // steps
    out_rows = ret_w_out.shape[0] // steps
    col_w = ret_qk // steps
    assert n_mod % tn_mod == 0 and tn_mod % LANES == 0 and out_rows % 8 == 0
    assert ret_qk == moba_qk == steps * col_w and col_w % LANES == 0
    mod_tiles = n_mod // tn_mod

    inv_r = RET_THETA ** (-jnp.arange(half_r, dtype=F32) * 2.0 / ret_dk)
    inv_m = ROPE_THETA ** (-jnp.arange(half_m, dtype=F32) * 2.0 / (2 * half_m))

    mod_idx = lambda t: (t // mod_tiles, 0, t % mod_tiles)
    tab_idx = lambda t: (t // seq_tiles, t % seq_tiles, 0)
    tabt_idx = lambda t: (t // seq_tiles, 0, t % seq_tiles)
    col_tile = pl.BlockSpec((d, col_w), lambda t: (0, t))
    row_tile = pl.BlockSpec((out_rows, ret_w_out.shape[1]), lambda t: (t, 0))
    tab = jax.ShapeDtypeStruct((b, s, LANES), F32)
    tabt = jax.ShapeDtypeStruct((b, LANES, s), F32)
    outs = pl.pallas_call(
        functools.partial(_setup_kernel, half_m=half_m),
        grid=(steps,),
        in_specs=[
            pl.BlockSpec((b, d), lambda t: (0, 0)),
            pl.BlockSpec((None, d, tn_mod), mod_idx),
            pl.BlockSpec((None, 1, tn_mod), mod_idx),
            pl.BlockSpec((1, ts, 1), tab_idx),
            pl.BlockSpec((1, 1, ts), tabt_idx),
            pl.BlockSpec((1, LANES), lambda t: (0, 0)),
            pl.BlockSpec((half_m, 1), lambda t: (0, 0)),
            col_tile, row_tile, col_tile,
        ],
        out_specs=[pl.BlockSpec((None, b, tn_mod), mod_idx)]
        + [pl.BlockSpec((1, ts, LANES), tab_idx)] * 2
        + [pl.BlockSpec((1, LANES, ts), tabt_idx)] * 2
        + [col_tile, row_tile, col_tile],
        out_shape=[jax.ShapeDtypeStruct((depth, b, n_mod), F32), tab, tab, tabt, tabt,
                   jax.ShapeDtypeStruct((d, ret_qk), BF16),
                   jax.ShapeDtypeStruct(ret_w_out.shape, BF16),
                   jax.ShapeDtypeStruct((d, moba_qk), BF16)],
        compiler_params=_params(("parallel",)),
        name="setup",
    )(c, w_ada, b_ada.reshape(depth, 1, n_mod), positions.reshape(b, s, 1),
      positions.reshape(b, 1, s), inv_r.reshape(1, LANES), inv_m.reshape(half_m, 1),
      ret_w_in, ret_w_out, moba_w_in)
    return outs[0], outs[1:5], outs[5], outs[6], outs[7]


def _inproj_qk_kernel(x_ref, shift_ref, scale_ref, w_ref, cos_ref, sin_ref, o_ref, h_ref,
                      *, tn, head_w, k_mult):
    h = x_ref[...] * (1.0 + scale_ref[0]) + shift_ref[0]
    h_ref[...] = h.astype(h_ref.dtype)
    n = w_ref.shape[1]
    cos_t, sin_t = cos_ref[...], sin_ref[...]
    for c0 in range(0, n, tn):
        acc = jnp.dot(h_ref[...], w_ref[:, c0:c0 + tn], preferred_element_type=F32)
        is_q = c0 < n // 2
        c = cos_t if is_q else cos_t * k_mult
        s = sin_t if is_q else sin_t * k_mult
        for lo in range(0, tn, head_w):
            x1 = acc[:, lo:lo + LANES]
            x2 = acc[:, lo + LANES:lo + 2 * LANES]
            o_ref[:, c0 + lo:c0 + lo + LANES] = (x1 * c - x2 * s).astype(o_ref.dtype)
            o_ref[:, c0 + lo + LANES:c0 + lo + 2 * LANES] = (x2 * c + x1 * s).astype(o_ref.dtype)


def _inproj_qk(x2d, shift, scale, w_qk, cos_t, sin_t, *, seq, head_w, k_mult, tm=512, tn=1024):
    m, d = x2d.shape
    n = w_qk.shape[1]
    assert seq % tm == 0 and (n // 2) % tn == 0 and tn % head_w == 0 and head_w == 2 * LANES
    tiles_per_seq = seq // tm
    tab_spec = pl.BlockSpec((tm, LANES), lambda i: (i, 0))
    mod_spec = pl.BlockSpec((1, 1, d), lambda i: (i // tiles_per_seq, 0, 0))
    return pl.pallas_call(
        functools.partial(_inproj_qk_kernel, tn=tn, head_w=head_w, k_mult=k_mult),
        grid=(m // tm,),
        in_specs=[
            pl.BlockSpec((tm, d), lambda i: (i, 0)),
            mod_spec, mod_spec,
            pl.BlockSpec((d, n), lambda i: (0, 0), pipeline_mode=pl.Buffered(1)),
            tab_spec, tab_spec,
        ],
        out_specs=[pl.BlockSpec((tm, n), lambda i: (i, 0)),
                   pl.BlockSpec((tm, d), lambda i: (i, 0))],
        out_shape=[jax.ShapeDtypeStruct((m, n), BF16), jax.ShapeDtypeStruct((m, d), BF16)],
        compiler_params=_params(("parallel",)),
        name="inproj_qk",
    )(x2d, shift, scale, w_qk, cos_t, sin_t)


def _matmul_kernel(h_ref, w_ref, o_ref):
    o_ref[...] = jnp.dot(h_ref[...], w_ref[...].astype(h_ref.dtype),
                         preferred_element_type=F32).astype(o_ref.dtype)


def _inproj_plain(h2d, w_mat, *, col0=0, tm=2048, tn=1024):
    m, d = h2d.shape
    n = w_mat.shape[1] - col0
    assert m % tm == 0 and n % tn == 0 and col0 % tn == 0
    j0 = col0 // tn
    return pl.pallas_call(
        _matmul_kernel,
        grid=(m // tm, n // tn),
        in_specs=[pl.BlockSpec((tm, d), lambda i, j: (i, 0)),
                  pl.BlockSpec((d, tn), lambda i, j: (0, j0 + j))],
        out_specs=pl.BlockSpec((tm, tn), lambda i, j: (i, j)),
        out_shape=jax.ShapeDtypeStruct((m, n), BF16),
        compiler_params=_params(("parallel", "parallel")),
        name="inproj_plain",
    )(h2d, w_mat)


def _silu(g):
    half = 0.5 * g
    return half + half * jnp.tanh(half)


def _layernorm_via(o_ref, rows, z, lng, lnb):
    o_ref[rows] = z
    mu = jnp.mean(o_ref[rows], axis=-1, keepdims=True)
    var = jnp.mean(jnp.square(o_ref[rows] - mu), axis=-1, keepdims=True)
    xn = (o_ref[rows] - mu) * lax.rsqrt(var + LN_EPS) * lng + lnb
    o_ref[rows] = xn
    return xn


def _retention_outproj_kernel(q_ref, k_ref, v_ref, g_ref, dec_ref, qd_ref, kd_ref, cd_ref,
                              w_ref, x_ref, gate_ref, lng_ref, lnb_ref, nshift_ref, nscale_ref,
                              o_ref, h_ref, st_scr, *, heads, dk, dv):
    @pl.when(pl.program_id(1) == 0)
    def _():
        st_scr[...] = jnp.zeros_like(st_scr)

    def retain(hh):
        qc = q_ref[0, :, hh * dk:(hh + 1) * dk]
        kc = k_ref[0, :, hh * dk:(hh + 1) * dk]
        vc = v_ref[0, :, hh * dv:(hh + 1) * dv]
        scores = lax.dot_general(qc, kc, (((1,), (1,)), ((), ())),
                                 preferred_element_type=F32) * dec_ref[hh]
        inner = jnp.dot(scores.astype(BF16), vc, preferred_element_type=F32)
        kdec = (kc.astype(F32) * kd_ref[hh]).astype(BF16)
        update = lax.dot_general(kdec, vc, (((0,), (0,)), ((), ())), preferred_element_type=F32)
        state = st_scr[hh]
        cross = jnp.dot(qc, state.astype(BF16), preferred_element_type=F32) * qd_ref[hh]
        st_scr[hh] = state * cd_ref[hh] + update
        return inner + cross

    def gate_norm(hh, o):
        mu = jnp.mean(o, axis=-1, keepdims=True)
        oc = o - mu
        var = jnp.mean(oc * oc, axis=-1, keepdims=True)
        gn = oc * lax.rsqrt(var + RET_GN_EPS)
        g = g_ref[0, :, hh * dv:(hh + 1) * dv].astype(F32)
        return (gn * _silu(g)).astype(BF16)

    ahead = 2
    o = {hh: retain(hh) for hh in range(min(ahead, heads))}
    u = gate_norm(0, o.pop(0))
    y = None
    for hh in range(heads):
        if hh + ahead < heads:
            o[hh + ahead] = retain(hh + ahead)
        part = jnp.dot(u, w_ref[hh * dv:(hh + 1) * dv, :], preferred_element_type=F32)
        y = part if y is None else part + y
        if hh + 1 < heads:
            u = gate_norm(hh + 1, o.pop(hh + 1))

    xn = _layernorm_via(o_ref, 0, DEEPNORM_ALPHA * x_ref[0] + gate_ref[0] * y,
                        lng_ref[...], lnb_ref[...])
    h_ref[0] = (xn * (1.0 + nscale_ref[0]) + nshift_ref[0]).astype(h_ref.dtype)


def _retention_decays(heads, chunk):
    log_gamma = jnp.log(1.0 - 2.0 ** (-5.0 - jnp.arange(heads, dtype=F32)))
    n = jnp.arange(chunk, dtype=F32)
    diff = n[:, None] - n[None, :]
    inner = jnp.where(diff >= 0, jnp.exp(jnp.maximum(diff, 0.0) * log_gamma[:, None, None]), 0.0)
    q_decay = jnp.exp((n + 1.0) * log_gamma[:, None])
    k_decay = jnp.exp((chunk - 1.0 - n) * log_gamma[:, None])
    chunk_decay = jnp.exp(chunk * log_gamma)
    return (inner.astype(F32), q_decay.reshape(heads, chunk, 1), k_decay.reshape(heads, chunk, 1),
            chunk_decay.reshape(heads, 1, 1))


def _retention_outproj(qk, vg, w_out, x, gate, ln_g, ln_b, next_shift, next_scale,
                       *, heads, dk, dv):
    b, s, d = x.shape
    chunk = RET_CHUNK
    assert s % chunk == 0 and w_out.shape == (heads * dv, d)
    dec, qd, kd, cd = _retention_decays(heads, chunk)
    whole = lambda shape: pl.BlockSpec(shape, lambda i, c: (0,) * len(shape))
    rows = lambda width, blk: pl.BlockSpec((1, chunk, width), lambda i, c: (i, c, blk))
    per_batch = pl.BlockSpec((1, 1, d), lambda i, c: (i, 0, 0))
    return pl.pallas_call(
        functools.partial(_retention_outproj_kernel, heads=heads, dk=dk, dv=dv),
        grid=(b, s // chunk),
        in_specs=[
            rows(heads * dk, 0), rows(heads * dk, 1), rows(heads * dv, 0), rows(heads * dv, 1),
            whole((heads, chunk, chunk)), whole((heads, chunk, 1)), whole((heads, chunk, 1)),
            whole((heads, 1, 1)),
            pl.BlockSpec((heads * dv, d), lambda i, c: (0, 0), pipeline_mode=pl.Buffered(1)),
            rows(d, 0), per_batch, whole((1, d)), whole((1, d)), per_batch, per_batch,
        ],
        out_specs=[rows(d, 0), rows(d, 0)],
        out_shape=[jax.ShapeDtypeStruct((b, s, d), F32), jax.ShapeDtypeStruct((b, s, d), BF16)],
        scratch_shapes=[pltpu.VMEM((heads, dk, dv), F32)],
        compiler_params=_params(("parallel", "arbitrary")),
        name="retention_outproj",
    )(qk, qk, vg, vg, dec, qd, kd, cd, w_out, x, gate, ln_g.reshape(1, d), ln_b.reshape(1, d),
      next_shift, next_scale)


def _moba_kernel(q_ref, k_ref, v_ref, g_ref, mct_ref, mst_ref, o_ref,
                 mc_scr, ms_scr, qr_scrs, kr_scrs, vt_scrs, km_scrs, s_scrs, *,
                 n_blocks, blk, sub, topk, exp_scale, dh):
    @pl.when(pl.program_id(1) == 0)
    def _():
        mc_scr[...] = mct_ref[0].T
        ms_scr[...] = mst_ref[0].T

    n_heads = q_ref.shape[2] // dh
    programs = [_moba_head_program(hh, q_ref, k_ref, v_ref, g_ref, mc_scr, ms_scr, o_ref,
                                   qr_scrs.at[hh], kr_scrs.at[hh], vt_scrs.at[hh], km_scrs.at[hh],
                                   s_scrs.at[hh], n_blocks=n_blocks, blk=blk, sub=sub, topk=topk,
                                   exp_scale=exp_scale, dh=dh)
                for hh in range(n_heads)]
    while programs:
        programs = [p for p in programs if next(p, "done") != "done"]


def _moba_head_program(hh, q_ref, k_ref, v_ref, g_ref, mc_scr, ms_scr, o_ref,
                       qr_scr, kr_scr, vt_scr, km_scr, s_slots, *,
                       n_blocks, blk, sub, topk, exp_scale, dh):
    lanes = slice(hh * dh, (hh + 1) * dh)
    vt_scr[dh:, :] = jnp.ones((vt_scr.shape[0] - dh, vt_scr.shape[1]), BF16)
    km_scr[...] = jnp.zeros_like(km_scr)

    def prepare(b):
        rows = slice(b * blk, (b + 1) * blk)
        mc = mc_scr[rows, :]
        ms = ms_scr[rows, :]

        def rotary(x_ref):
            x = x_ref[0, rows, lanes].astype(F32)
            return x * mc + pltpu.roll(x, LANES // 2, 1) * ms

        qr_scr[rows, :] = (rotary(q_ref) * exp_scale).astype(BF16)
        kr = rotary(k_ref)
        kr_scr[rows, :] = kr.astype(BF16)
        km_scr[b:b + 1, :] = jnp.mean(kr, axis=0, keepdims=True)
        vt_scr[:dh, rows] = v_ref[0, rows, lanes].T

    order = list(range(1, n_blocks, 2)) + list(range((n_blocks - 1) // 2 * 2, -1, -2))
    slot_of = {i: pos % 2 for pos, i in enumerate(order)}

    nt = (((1,), (1,)), ((), ()))
    key_id = lax.broadcasted_iota(jnp.int32, (sub, blk), 0)
    qry_id = lax.broadcasted_iota(jnp.int32, (sub, blk), 1)
    blk_id = lax.broadcasted_iota(jnp.int32, (n_blocks, blk), 0)

    def scores(i):
        cols = slice(i * blk, (i + 1) * blk)
        s_scr = s_slots.at[slot_of[i]]
        qi = qr_scr[cols, :]
        ranked = i > topk
        if ranked:
            gi = lax.dot_general(km_scr[...].astype(BF16), qi, nt,
                                 preferred_element_type=F32)
            past = blk_id < i
        m = None
        bias = []
        for j in range(i + 1):
            mj = None
            for part in range(blk // sub):
                keys = slice(j * blk + part * sub, j * blk + (part + 1) * sub)
                sj = lax.dot_general(kr_scr[keys, :], qi, nt,
                                     preferred_element_type=F32)
                if j == i:
                    sj = jnp.where(key_id + part * sub <= qry_id, sj, NEG)
                s_scr[keys, :] = sj
                mp = jnp.max(sj, axis=0, keepdims=True)
                mj = mp if mj is None else jnp.maximum(mj, mp)
            if ranked and j < i:
                gj = gi[j:j + 1, :]
                beats = past & ((gi > gj) | ((gi == gj) & (blk_id < j)))
                cnt = jnp.sum(jnp.where(beats, 1.0, 0.0), axis=0, keepdims=True)
                bias.append(jnp.where(cnt < topk, 0.0, NEG))
                mj = mj + bias[-1]
            m = mj if m is None else jnp.maximum(m, mj)
        if not ranked:
            return [m]
        return [m - bj for bj in bias] + [m]

    def attend(i, subs):
        cols = slice(i * blk, (i + 1) * blk)
        ot = None
        for j in range(i + 1):
            keys = slice(j * blk, (j + 1) * blk)
            e = jnp.exp2(s_slots[slot_of[i], keys, :]
                         - subs[min(j, len(subs) - 1)]).astype(BF16)
            pj = jnp.dot(vt_scr[:, keys], e, preferred_element_type=F32)
            ot = pj if ot is None else pj + ot
        o = (ot[:dh, :] * (1.0 / ot[dh:dh + 1, :])).T
        g = g_ref[0, cols, lanes].astype(F32)
        o_ref[0, cols, lanes] = (o * _silu(g)).astype(o_ref.dtype)

    prepared = 0

    def prepare_upto(b):
        nonlocal prepared
        for blk_i in range(prepared, b + 1):
            prepare(blk_i)
        prepared = max(prepared, b + 1)

    prepare_upto(order[0])
    yield
    nxt = scores(order[0])
    yield
    for pos, i in enumerate(order):
        cur = nxt
        if pos + 2 < n_blocks:
            prepare_upto(order[pos + 2])
        if pos + 1 < n_blocks:
            prepare_upto(order[pos + 1])
            nxt = scores(order[pos + 1])
            yield
        attend(i, cur)
        yield


def _moba_core(qk, vg, mct, mst, *, heads, dh, heads_per_step=4):
    b, s, _ = qk.shape
    blk = MOBA_BLOCK
    hp = heads_per_step
    assert s % blk == 0 and dh == LANES and heads % hp == 0
    n_blocks = s // blk
    groups = heads // hp
    spec = lambda off: pl.BlockSpec((1, s, hp * dh), lambda i, h: (i, 0, off + h))
    tab = pl.BlockSpec((1, dh, s), lambda i, h: (i, 0, 0))
    return pl.pallas_call(
        functools.partial(_moba_kernel, n_blocks=n_blocks, blk=blk, sub=MOBA_SCORE_ROWS,
                          topk=min(MOBA_TOPK, n_blocks),
                          exp_scale=dh ** -0.5 * math.log2(math.e), dh=dh),
        grid=(b, groups),
        in_specs=[spec(0), spec(groups), spec(0), spec(groups), tab, tab],
        out_specs=spec(0),
        out_shape=jax.ShapeDtypeStruct((b, s, heads * dh), BF16),
        scratch_shapes=[pltpu.VMEM((s, dh), F32), pltpu.VMEM((s, dh), F32),
                        pltpu.VMEM((hp, s, dh), BF16), pltpu.VMEM((hp, s, dh), BF16),
                        pltpu.VMEM((hp, dh + BF16_SUBLANES, s), BF16),
                        pltpu.VMEM((hp, n_blocks, dh), F32),
                        pltpu.VMEM((hp, 2, s, blk), F32)],
        compiler_params=_params(("parallel", "arbitrary")),
        name="moba_core",
    )(qk, qk, vg, vg, mct, mst)


def _outproj_ln_kernel(u_ref, w_ref, x_ref, gate_ref, lng_ref, lnb_ref, o_ref, *, sub):
    n_sub = u_ref.shape[0] // sub

    def matmul(r):
        return jnp.dot(u_ref[r * sub:(r + 1) * sub, :], w_ref[...].astype(u_ref.dtype),
                       preferred_element_type=F32)

    y_next = matmul(0)
    for r in range(n_sub):
        rows = slice(r * sub, (r + 1) * sub)
        y = y_next
        if r + 1 < n_sub:
            y_next = matmul(r + 1)
        _layernorm_via(o_ref, rows, DEEPNORM_ALPHA * x_ref[rows, :] + gate_ref[0] * y,
                       lng_ref[...], lnb_ref[...])


def _outproj_ln(u2d, w_mat, x2d, gate, ln_g, ln_b, *, seq, tm, sub=256):
    m, kdim = u2d.shape
    d = w_mat.shape[1]
    assert seq % tm == 0 and tm % sub == 0
    tiles_per_seq = seq // tm
    row = pl.BlockSpec((1, d), lambda i: (0, 0))
    tile = pl.BlockSpec((tm, d), lambda i: (i, 0))
    return pl.pallas_call(
        functools.partial(_outproj_ln_kernel, sub=sub),
        grid=(m // tm,),
        in_specs=[
            pl.BlockSpec((tm, kdim), lambda i: (i, 0)),
            pl.BlockSpec((kdim, d), lambda i: (0, 0), pipeline_mode=pl.Buffered(1)),
            tile,
            pl.BlockSpec((1, 1, d), lambda i: (i // tiles_per_seq, 0, 0)),
            row, row,
        ],
        out_specs=tile,
        out_shape=jax.ShapeDtypeStruct((m, d), F32),
        compiler_params=_params(("parallel",)),
        name="outproj_ln",
    )(u2d, w_mat, x2d, gate, ln_g.reshape(1, d), ln_b.reshape(1, d))


def kernel(x, c, positions, ret_w_in, ret_w_out, moba_w_in, moba_w_out, w_ada, b_ada, ln_g, ln_b):
    b, s, d = x.shape
    assert DEPTH == 2 and w_ada.shape[0] == DEPTH
    ret_dk = d // RET_HEADS
    ret_dv = 2 * d // RET_HEADS
    moba_dh = d // MOBA_HEADS
    ret_qk = 2 * RET_HEADS * ret_dk
    moba_qk = 2 * MOBA_HEADS * moba_dh

    mod, (cos_r, sin_r, mct, mst), ret_w_qk, ret_w_o, moba_w_qk = _setup(
        c, w_ada, b_ada, positions, ret_w_in[0], ret_w_out[0], moba_w_in[0],
        ret_dk=ret_dk, ret_qk=ret_qk, moba_dh=moba_dh, moba_qk=moba_qk)
    shift, scale, gate = [[mod[l, :, k * d:(k + 1) * d].reshape(b, 1, d) for l in range(DEPTH)]
                          for k in range(3)]

    x2d = x.reshape(b * s, d)
    qk, h2d = _inproj_qk(x2d, shift[0], scale[0], ret_w_qk,
                         cos_r.reshape(b * s, LANES), sin_r.reshape(b * s, LANES),
                         seq=s, head_w=ret_dk, k_mult=ret_dk ** -0.5)
    vg = _inproj_plain(h2d, ret_w_in[0], col0=ret_qk)
    x1, h1 = _retention_outproj(qk.reshape(b, s, -1), vg.reshape(b, s, -1), ret_w_o, x, gate[0],
                                ln_g[0], ln_b[0], shift[1], scale[1],
                                heads=RET_HEADS, dk=ret_dk, dv=ret_dv)
    x2d, h2d = x1.reshape(b * s, d), h1.reshape(b * s, d)

    qk = _inproj_plain(h2d, moba_w_qk)
    vg = _inproj_plain(h2d, moba_w_in[0], col0=moba_qk)
    u = _moba_core(qk.reshape(b, s, -1), vg.reshape(b, s, -1), mct, mst,
                   heads=MOBA_HEADS, dh=moba_dh)
    x2d = _outproj_ln(u.reshape(b * s, -1), moba_w_out[0], x2d, gate[1],
                      ln_g[1], ln_b[1], seq=s, tm=512)
    return x2d.reshape(b, s, d)
```

```python
import functools
import math

import jax
import jax.numpy as jnp
from jax import lax
from jax.experimental import pallas as pl
from jax.experimental.pallas import tpu as pltpu

F32 = jnp.float32
BF16 = jnp.bfloat16

DEPTH = 2
DEEPNORM_ALPHA = (2.0 * DEPTH) ** 0.25
LN_EPS = 1e-5

RET_HEADS = 8
RET_CHUNK = 256
RET_THETA = 10000.0
RET_GN_EPS = 1e-5

MOBA_HEADS = 16
MOBA_BLOCK = 256
MOBA_TOPK = 3
MOBA_SCORE_ROWS = 256
ROPE_THETA = 500000.0
NEG = -1e30

LANES = 128
BF16_SUBLANES = 16
VMEM_LIMIT = 60 * 1024 * 1024


def _params(sem):
    return pltpu.CompilerParams(dimension_semantics=sem, vmem_limit_bytes=VMEM_LIMIT)


def _setup_kernel(c_ref, wada_ref, bada_ref, pos_ref, posrow_ref, inv_r_ref, inv_m_ref,
                  rwin_ref, rwout_ref, mwin_ref, mwout_ref,
                  mod_ref, cos_r, sin_r, mct, mst, rwin_o, rwout_o, mwin_o, mwout_o, *, half_m):
    acc = jnp.dot(c_ref[...].astype(BF16), wada_ref[...].astype(BF16),
                  preferred_element_type=F32)
    mod_ref[...] = acc + bada_ref[...]

    rwin_o[...] = rwin_ref[...].astype(rwin_o.dtype)
    rwout_o[...] = rwout_ref[...].astype(rwout_o.dtype)
    mwout_o[...] = mwout_ref[...].astype(mwout_o.dtype)

    mid = LANES // 2
    for lo in range(0, mwin_ref.shape[1], LANES):
        x = mwin_ref[:, lo:lo + LANES]
        wlane = lax.broadcasted_iota(jnp.int32, x.shape, 1)
        up = pltpu.roll(x, mid - half_m, 1)
        dn = pltpu.roll(x, LANES - (mid - half_m), 1)
        y = jnp.where((wlane >= half_m) & (wlane < 2 * half_m), dn,
                      jnp.where((wlane >= mid) & (wlane < mid + half_m), up, x))
        mwin_o[:, lo:lo + LANES] = y.astype(mwin_o.dtype)

    pos = pos_ref[0].astype(F32)
    ang = pos * inv_r_ref[...]
    cos_r[0] = jnp.cos(ang)
    sin_r[0] = jnp.sin(ang)
    angm = inv_m_ref[...] * posrow_ref[0].astype(F32)
    cm = jnp.cos(angm)
    sm = jnp.sin(angm)
    gap = LANES // 2 - half_m
    ones = jnp.ones((gap, angm.shape[1]), F32)
    zeros = jnp.zeros((gap, angm.shape[1]), F32)
    mct[0] = jnp.concatenate([cm, ones, cm, ones], axis=0)
    mst[0] = jnp.concatenate([-sm, zeros, sm, zeros], axis=0)


def _setup(c, w_ada, b_ada, positions, ret_w_in, ret_w_out, moba_w_in, moba_w_out, *,
           ret_dk, ret_qk, moba_dh, moba_qk, ts=1024):
    depth, d, n_mod = w_ada.shape
    b, s = positions.shape
    steps = b * s // ts
    seq_tiles = s // ts
    half_r = ret_dk // 2
    half_m = moba_dh // 8
    assert half_r == LANES and moba_dh == LANES and LANES % half_m == 0
    tn_mod = depth * n_mod // steps
    out_rows = ret_w_out.shape[0] // steps
    col_w = ret_qk // steps
    assert n_mod % tn_mod == 0 and tn_mod % LANES == 0 and out_rows % 8 == 0
    assert ret_qk == moba_qk == steps * col_w and col_w % LANES == 0
    mod_tiles = n_mod // tn_mod

    inv_r = RET_THETA ** (-jnp.arange(half_r, dtype=F32) * 2.0 / ret_dk)
    inv_m = ROPE_THETA ** (-jnp.arange(half_m, dtype=F32) * 2.0 / (2 * half_m))

    mod_idx = lambda t: (t // mod_tiles, 0, t % mod_tiles)
    tab_idx = lambda t: (t // seq_tiles, t % seq_tiles, 0)
    tabt_idx = lambda t: (t // seq_tiles, 0, t % seq_tiles)
    col_tile = pl.BlockSpec((d, col_w), lambda t: (0, t))
    row_tile = pl.BlockSpec((out_rows, ret_w_out.shape[1]), lambda t: (t, 0))
    mout_rows = moba_w_out.shape[0] // steps
    assert mout_rows % 8 == 0
    mrow_tile = pl.BlockSpec((mout_rows, moba_w_out.shape[1]), lambda t: (t, 0))
    tab = jax.ShapeDtypeStruct((b, s, LANES), F32)
    tabt = jax.ShapeDtypeStruct((b, LANES, s), F32)
    outs = pl.pallas_call(
        functools.partial(_setup_kernel, half_m=half_m),
        grid=(steps,),
        in_specs=[
            pl.BlockSpec((b, d), lambda t: (0, 0)),
            pl.BlockSpec((None, d, tn_mod), mod_idx),
            pl.BlockSpec((None, 1, tn_mod), mod_idx),
            pl.BlockSpec((1, ts, 1), tab_idx),
            pl.BlockSpec((1, 1, ts), tabt_idx),
            pl.BlockSpec((1, LANES), lambda t: (0, 0)),
            pl.BlockSpec((half_m, 1), lambda t: (0, 0)),
            col_tile, row_tile, col_tile, mrow_tile,
        ],
        out_specs=[pl.BlockSpec((None, b, tn_mod), mod_idx)]
        + [pl.BlockSpec((1, ts, LANES), tab_idx)] * 2
        + [pl.BlockSpec((1, LANES, ts), tabt_idx)] * 2
        + [col_tile, row_tile, col_tile, mrow_tile],
        out_shape=[jax.ShapeDtypeStruct((depth, b, n_mod), F32), tab, tab, tabt, tabt,
                   jax.ShapeDtypeStruct((d, ret_qk), BF16),
                   jax.ShapeDtypeStruct(ret_w_out.shape, BF16),
                   jax.ShapeDtypeStruct((d, moba_qk), BF16),
                   jax.ShapeDtypeStruct(moba_w_out.shape, BF16)],
        compiler_params=_params(("parallel",)),
        name="setup",
    )(c, w_ada, b_ada.reshape(depth, 1, n_mod), positions.reshape(b, s, 1),
      positions.reshape(b, 1, s), inv_r.reshape(1, LANES), inv_m.reshape(half_m, 1),
      ret_w_in, ret_w_out, moba_w_in, moba_w_out)
    return outs[0], outs[1:5], outs[5], outs[6], outs[7], outs[8]


def _inproj_qk_kernel(x_ref, shift_ref, scale_ref, w_ref, cos_ref, sin_ref, o_ref, h_ref,
                      *, tn, head_w, k_mult):
    h = x_ref[...] * (1.0 + scale_ref[0]) + shift_ref[0]
    h_ref[...] = h.astype(h_ref.dtype)
    n = w_ref.shape[1]
    cos_t, sin_t = cos_ref[...], sin_ref[...]
    for c0 in range(0, n, tn):
        acc = jnp.dot(h_ref[...], w_ref[:, c0:c0 + tn], preferred_element_type=F32)
        is_q = c0 < n // 2
        c = cos_t if is_q else cos_t * k_mult
        s = sin_t if is_q else sin_t * k_mult
        for lo in range(0, tn, head_w):
            x1 = acc[:, lo:lo + LANES]
            x2 = acc[:, lo + LANES:lo + 2 * LANES]
            o_ref[:, c0 + lo:c0 + lo + LANES] = (x1 * c - x2 * s).astype(o_ref.dtype)
            o_ref[:, c0 + lo + LANES:c0 + lo + 2 * LANES] = (x2 * c + x1 * s).astype(o_ref.dtype)


def _inproj_qk(x2d, shift, scale, w_qk, cos_t, sin_t, *, seq, head_w, k_mult, tm=512, tn=1024):
    m, d = x2d.shape
    n = w_qk.shape[1]
    assert seq % tm == 0 and (n // 2) % tn == 0 and tn % head_w == 0 and head_w == 2 * LANES
    tiles_per_seq = seq // tm
    tab_spec = pl.BlockSpec((tm, LANES), lambda i: (i, 0))
    mod_spec = pl.BlockSpec((1, 1, d), lambda i: (i // tiles_per_seq, 0, 0))
    return pl.pallas_call(
        functools.partial(_inproj_qk_kernel, tn=tn, head_w=head_w, k_mult=k_mult),
        grid=(m // tm,),
        in_specs=[
            pl.BlockSpec((tm, d), lambda i: (i, 0)),
            mod_spec, mod_spec,
            pl.BlockSpec((d, n), lambda i: (0, 0), pipeline_mode=pl.Buffered(1)),
            tab_spec, tab_spec,
        ],
        out_specs=[pl.BlockSpec((tm, n), lambda i: (i, 0)),
                   pl.BlockSpec((tm, d), lambda i: (i, 0))],
        out_shape=[jax.ShapeDtypeStruct((m, n), BF16), jax.ShapeDtypeStruct((m, d), BF16)],
        compiler_params=_params(("parallel",)),
        name="inproj_qk",
    )(x2d, shift, scale, w_qk, cos_t, sin_t)


def _matmul_kernel(h_ref, w_ref, o_ref):
    o_ref[...] = jnp.dot(h_ref[...], w_ref[...].astype(h_ref.dtype),
                         preferred_element_type=F32).astype(o_ref.dtype)


def _inproj_plain(h2d, w_mat, *, col0=0, tm=2048, tn=1024):
    m, d = h2d.shape
    n = w_mat.shape[1] - col0
    assert m % tm == 0 and n % tn == 0 and col0 % tn == 0
    j0 = col0 // tn
    return pl.pallas_call(
        _matmul_kernel,
        grid=(m // tm, n // tn),
        in_specs=[pl.BlockSpec((tm, d), lambda i, j: (i, 0)),
                  pl.BlockSpec((d, tn), lambda i, j: (0, j0 + j))],
        out_specs=pl.BlockSpec((tm, tn), lambda i, j: (i, j)),
        out_shape=jax.ShapeDtypeStruct((m, n), BF16),
        compiler_params=_params(("parallel", "parallel")),
        name="inproj_plain",
    )(h2d, w_mat)


def _silu(g):
    half = 0.5 * g
    return half + half * jnp.tanh(half)


def _layernorm_via(o_ref, rows, z, lng, lnb):
    o_ref[rows] = z
    mu = jnp.mean(o_ref[rows], axis=-1, keepdims=True)
    var = jnp.mean(jnp.square(o_ref[rows] - mu), axis=-1, keepdims=True)
    xn = (o_ref[rows] - mu) * lax.rsqrt(var + LN_EPS) * lng + lnb
    o_ref[rows] = xn
    return xn


def _retention_outproj_kernel(q_ref, k_ref, v_ref, g_ref, dec_ref, qd_ref, kd_ref, cd_ref,
                              w_ref, x_ref, gate_ref, lng_ref, lnb_ref, nshift_ref, nscale_ref,
                              o_ref, h_ref, st_scr, *, heads, dk, dv):
    @pl.when(pl.program_id(1) == 0)
    def _():
        st_scr[...] = jnp.zeros_like(st_scr)

    def retain(hh):
        qc = q_ref[0, :, hh * dk:(hh + 1) * dk]
        kc = k_ref[0, :, hh * dk:(hh + 1) * dk]
        vc = v_ref[0, :, hh * dv:(hh + 1) * dv]
        scores = lax.dot_general(qc, kc, (((1,), (1,)), ((), ())),
                                 preferred_element_type=F32) * dec_ref[hh]
        inner = jnp.dot(scores.astype(BF16), vc, preferred_element_type=F32)
        kdec = (kc.astype(F32) * kd_ref[hh]).astype(BF16)
        update = lax.dot_general(kdec, vc, (((0,), (0,)), ((), ())), preferred_element_type=F32)
        state = st_scr[hh]
        cross = jnp.dot(qc, state.astype(BF16), preferred_element_type=F32) * qd_ref[hh]
        st_scr[hh] = state * cd_ref[hh] + update
        return inner + cross

    def gate_norm(hh, o):
        mu = jnp.mean(o, axis=-1, keepdims=True)
        oc = o - mu
        var = jnp.mean(oc * oc, axis=-1, keepdims=True)
        gn = oc * lax.rsqrt(var + RET_GN_EPS)
        g = g_ref[0, :, hh * dv:(hh + 1) * dv].astype(F32)
        return (gn * _silu(g)).astype(BF16)

    ahead = 2
    o = {hh: retain(hh) for hh in range(min(ahead, heads))}
    u = gate_norm(0, o.pop(0))
    y = None
    for hh in range(heads):
        if hh + ahead < heads:
            o[hh + ahead] = retain(hh + ahead)
        part = jnp.dot(u, w_ref[hh * dv:(hh + 1) * dv, :], preferred_element_type=F32)
        y = part if y is None else part + y
        if hh + 1 < heads:
            u = gate_norm(hh + 1, o.pop(hh + 1))

    xn = _layernorm_via(o_ref, 0, DEEPNORM_ALPHA * x_ref[0] + gate_ref[0] * y,
                        lng_ref[...], lnb_ref[...])
    h_ref[0] = (xn * (1.0 + nscale_ref[0]) + nshift_ref[0]).astype(h_ref.dtype)


def _retention_decays(heads, chunk):
    log_gamma = jnp.log(1.0 - 2.0 ** (-5.0 - jnp.arange(heads, dtype=F32)))
    n = jnp.arange(chunk, dtype=F32)
    diff = n[:, None] - n[None, :]
    inner = jnp.where(diff >= 0, jnp.exp(jnp.maximum(diff, 0.0) * log_gamma[:, None, None]), 0.0)
    q_decay = jnp.exp((n + 1.0) * log_gamma[:, None])
    k_decay = jnp.exp((chunk - 1.0 - n) * log_gamma[:, None])
    chunk_decay = jnp.exp(chunk * log_gamma)
    return (inner.astype(F32), q_decay.reshape(heads, chunk, 1), k_decay.reshape(heads, chunk, 1),
            chunk_decay.reshape(heads, 1, 1))


def _retention_outproj(qk, vg, w_out, x, gate, ln_g, ln_b, next_shift, next_scale,
                       *, heads, dk, dv):
    b, s, d = x.shape
    chunk = RET_CHUNK
    assert s % chunk == 0 and w_out.shape == (heads * dv, d)
    dec, qd, kd, cd = _retention_decays(heads, chunk)
    whole = lambda shape: pl.BlockSpec(shape, lambda i, c: (0,) * len(shape))
    rows = lambda width, blk: pl.BlockSpec((1, chunk, width), lambda i, c: (i, c, blk))
    per_batch = pl.BlockSpec((1, 1, d), lambda i, c: (i, 0, 0))
    return pl.pallas_call(
        functools.partial(_retention_outproj_kernel, heads=heads, dk=dk, dv=dv),
        grid=(b, s // chunk),
        in_specs=[
            rows(heads * dk, 0), rows(heads * dk, 1), rows(heads * dv, 0), rows(heads * dv, 1),
            whole((heads, chunk, chunk)), whole((heads, chunk, 1)), whole((heads, chunk, 1)),
            whole((heads, 1, 1)),
            pl.BlockSpec((heads * dv, d), lambda i, c: (0, 0), pipeline_mode=pl.Buffered(1)),
            rows(d, 0), per_batch, whole((1, d)), whole((1, d)), per_batch, per_batch,
        ],
        out_specs=[rows(d, 0), rows(d, 0)],
        out_shape=[jax.ShapeDtypeStruct((b, s, d), F32), jax.ShapeDtypeStruct((b, s, d), BF16)],
        scratch_shapes=[pltpu.VMEM((heads, dk, dv), F32)],
        compiler_params=_params(("parallel", "arbitrary")),
        name="retention_outproj",
    )(qk, qk, vg, vg, dec, qd, kd, cd, w_out, x, gate, ln_g.reshape(1, d), ln_b.reshape(1, d),
      next_shift, next_scale)


def _moba_kernel(q_ref, k_ref, v_ref, g_ref, mct_ref, mst_ref, o_ref,
                 mc_scr, ms_scr, qr_scrs, kr_scrs, vt_scrs, km_scrs, s_scrs, *,
                 n_blocks, blk, sub, topk, exp_scale, dh):
    @pl.when(pl.program_id(1) == 0)
    def _():
        mc_scr[...] = mct_ref[0].T
        ms_scr[...] = mst_ref[0].T

    n_heads = q_ref.shape[2] // dh
    programs = [_moba_head_program(hh, q_ref, k_ref, v_ref, g_ref, mc_scr, ms_scr, o_ref,
                                   qr_scrs.at[hh], kr_scrs.at[hh], vt_scrs.at[hh], km_scrs.at[hh],
                                   s_scrs.at[hh], n_blocks=n_blocks, blk=blk, sub=sub, topk=topk,
                                   exp_scale=exp_scale, dh=dh)
                for hh in range(n_heads)]
    while programs:
        programs = [p for p in programs if next(p, "done") != "done"]


def _moba_head_program(hh, q_ref, k_ref, v_ref, g_ref, mc_scr, ms_scr, o_ref,
                       qr_scr, kr_scr, vt_scr, km_scr, s_slots, *,
                       n_blocks, blk, sub, topk, exp_scale, dh):
    lanes = slice(hh * dh, (hh + 1) * dh)
    vt_scr[dh:, :] = jnp.ones((vt_scr.shape[0] - dh, vt_scr.shape[1]), BF16)
    km_scr[...] = jnp.zeros_like(km_scr)

    def prepare(b):
        rows = slice(b * blk, (b + 1) * blk)
        mc = mc_scr[rows, :]
        ms = ms_scr[rows, :]

        def rotary(x_ref):
            x = x_ref[0, rows, lanes].astype(F32)
            return x * mc + pltpu.roll(x, LANES // 2, 1) * ms

        qr_scr[rows, :] = (rotary(q_ref) * exp_scale).astype(BF16)
        kr = rotary(k_ref)
        kr_scr[rows, :] = kr.astype(BF16)
        km_scr[b:b + 1, :] = jnp.mean(kr, axis=0, keepdims=True)
        vt_scr[:dh, rows] = v_ref[0, rows, lanes].T

    order = list(range(1, n_blocks, 2)) + list(range((n_blocks - 1) // 2 * 2, -1, -2))
    slot_of = {i: pos % 2 for pos, i in enumerate(order)}

    nt = (((1,), (1,)), ((), ()))
    key_id = lax.broadcasted_iota(jnp.int32, (sub, blk), 0)
    qry_id = lax.broadcasted_iota(jnp.int32, (sub, blk), 1)
    blk_id = lax.broadcasted_iota(jnp.int32, (n_blocks, blk), 0)

    def scores(i):
        cols = slice(i * blk, (i + 1) * blk)
        s_scr = s_slots.at[slot_of[i]]
        qi = qr_scr[cols, :]
        ranked = i > topk
        if ranked:
            gi = lax.dot_general(km_scr[...].astype(BF16), qi, nt,
                                 preferred_element_type=F32)
            past = blk_id < i
        m = None
        bias = []
        for j in range(i + 1):
            mj = None
            for part in range(blk // sub):
                keys = slice(j * blk + part * sub, j * blk + (part + 1) * sub)
                sj = lax.dot_general(kr_scr[keys, :], qi, nt,
                                     preferred_element_type=F32)
                if j == i:
                    sj = jnp.where(key_id + part * sub <= qry_id, sj, NEG)
                s_scr[keys, :] = sj
                mp = jnp.max(sj, axis=0, keepdims=True)
                mj = mp if mj is None else jnp.maximum(mj, mp)
            if ranked and j < i:
                gj = gi[j:j + 1, :]
                beats = past & ((gi > gj) | ((gi == gj) & (blk_id < j)))
                cnt = jnp.sum(jnp.where(beats, 1.0, 0.0), axis=0, keepdims=True)
                bias.append(jnp.where(cnt < topk, 0.0, NEG))
                mj = mj + bias[-1]
            m = mj if m is None else jnp.maximum(m, mj)
        if not ranked:
            return [m]
        return [m - bj for bj in bias] + [m]

    def attend(i, subs):
        cols = slice(i * blk, (i + 1) * blk)
        ot = None
        for j in range(i + 1):
            keys = slice(j * blk, (j + 1) * blk)
            e = jnp.exp2(s_slots[slot_of[i], keys, :]
                         - subs[min(j, len(subs) - 1)]).astype(BF16)
            pj = jnp.dot(vt_scr[:, keys], e, preferred_element_type=F32)
            ot = pj if ot is None else pj + ot
        o = (ot[:dh, :] * (1.0 / ot[dh:dh + 1, :])).T
        g = g_ref[0, cols, lanes].astype(F32)
        o_ref[0, cols, lanes] = (o * _silu(g)).astype(o_ref.dtype)

    prepared = 0

    def prepare_upto(b):
        nonlocal prepared
        for blk_i in range(prepared, b + 1):
            prepare(blk_i)
        prepared = max(prepared, b + 1)

    prepare_upto(order[0])
    yield
    nxt = scores(order[0])
    yield
    for pos, i in enumerate(order):
        cur = nxt
        if pos + 2 < n_blocks:
            prepare_upto(order[pos + 2])
        if pos + 1 < n_blocks:
            prepare_upto(order[pos + 1])
            nxt = scores(order[pos + 1])
            yield
        attend(i, cur)
        yield


def _moba_core(qk, vg, mct, mst, *, heads, dh, heads_per_step=4):
    b, s, _ = qk.shape
    blk = MOBA_BLOCK
    hp = heads_per_step
    assert s % blk == 0 and dh == LANES and heads % hp == 0
    n_blocks = s // blk
    groups = heads // hp
    spec = lambda off: pl.BlockSpec((1, s, hp * dh), lambda i, h: (i, 0, off + h))
    tab = pl.BlockSpec((1, dh, s), lambda i, h: (i, 0, 0))
    return pl.pallas_call(
        functools.partial(_moba_kernel, n_blocks=n_blocks, blk=blk, sub=MOBA_SCORE_ROWS,
                          topk=min(MOBA_TOPK, n_blocks),
                          exp_scale=dh ** -0.5 * math.log2(math.e), dh=dh),
        grid=(b, groups),
        in_specs=[spec(0), spec(groups), spec(0), spec(groups), tab, tab],
        out_specs=spec(0),
        out_shape=jax.ShapeDtypeStruct((b, s, heads * dh), BF16),
        scratch_shapes=[pltpu.VMEM((s, dh), F32), pltpu.VMEM((s, dh), F32),
                        pltpu.VMEM((hp, s, dh), BF16), pltpu.VMEM((hp, s, dh), BF16),
                        pltpu.VMEM((hp, dh + BF16_SUBLANES, s), BF16),
                        pltpu.VMEM((hp, n_blocks, dh), F32),
                        pltpu.VMEM((hp, 2, s, blk), F32)],
        compiler_params=_params(("parallel", "arbitrary")),
        name="moba_core",
    )(qk, qk, vg, vg, mct, mst)


def _outproj_ln_kernel(u_ref, w_ref, x_ref, gate_ref, lng_ref, lnb_ref, o_ref, *, sub):
    n_sub = u_ref.shape[0] // sub

    def matmul(r):
        return jnp.dot(u_ref[r * sub:(r + 1) * sub, :], w_ref[...].astype(u_ref.dtype),
                       preferred_element_type=F32)

    y_next = matmul(0)
    for r in range(n_sub):
        rows = slice(r * sub, (r + 1) * sub)
        y = y_next
        if r + 1 < n_sub:
            y_next = matmul(r + 1)
        _layernorm_via(o_ref, rows, DEEPNORM_ALPHA * x_ref[rows, :] + gate_ref[0] * y,
                       lng_ref[...], lnb_ref[...])


def _outproj_ln(u2d, w_mat, x2d, gate, ln_g, ln_b, *, seq, tm, sub=256):
    m, kdim = u2d.shape
    d = w_mat.shape[1]
    assert seq % tm == 0 and tm % sub == 0
    tiles_per_seq = seq // tm
    row = pl.BlockSpec((1, d), lambda i: (0, 0))
    tile = pl.BlockSpec((tm, d), lambda i: (i, 0))
    return pl.pallas_call(
        functools.partial(_outproj_ln_kernel, sub=sub),
        grid=(m // tm,),
        in_specs=[
            pl.BlockSpec((tm, kdim), lambda i: (i, 0)),
            pl.BlockSpec((kdim, d), lambda i: (0, 0), pipeline_mode=pl.Buffered(1)),
            tile,
            pl.BlockSpec((1, 1, d), lambda i: (i // tiles_per_seq, 0, 0)),
            row, row,
        ],
        out_specs=tile,
        out_shape=jax.ShapeDtypeStruct((m, d), F32),
        compiler_params=_params(("parallel",)),
        name="outproj_ln",
    )(u2d, w_mat, x2d, gate, ln_g.reshape(1, d), ln_b.reshape(1, d))


def kernel(x, c, positions, ret_w_in, ret_w_out, moba_w_in, moba_w_out, w_ada, b_ada, ln_g, ln_b):
    b, s, d = x.shape
    assert DEPTH == 2 and w_ada.shape[0] == DEPTH
    ret_dk = d // RET_HEADS
    ret_dv = 2 * d // RET_HEADS
    moba_dh = d // MOBA_HEADS
    ret_qk = 2 * RET_HEADS * ret_dk
    moba_qk = 2 * MOBA_HEADS * moba_dh

    mod, (cos_r, sin_r, mct, mst), ret_w_qk, ret_w_o, moba_w_qk, moba_w_o = _setup(
        c, w_ada, b_ada, positions, ret_w_in[0], ret_w_out[0], moba_w_in[0], moba_w_out[0],
        ret_dk=ret_dk, ret_qk=ret_qk, moba_dh=moba_dh, moba_qk=moba_qk)
    shift, scale, gate = [[mod[l, :, k * d:(k + 1) * d].reshape(b, 1, d) for l in range(DEPTH)]
                          for k in range(3)]

    x2d = x.reshape(b * s, d)
    qk, h2d = _inproj_qk(x2d, shift[0], scale[0], ret_w_qk,
                         cos_r.reshape(b * s, LANES), sin_r.reshape(b * s, LANES),
                         seq=s, head_w=ret_dk, k_mult=ret_dk ** -0.5)
    vg = _inproj_plain(h2d, ret_w_in[0], col0=ret_qk)
    x1, h1 = _retention_outproj(qk.reshape(b, s, -1), vg.reshape(b, s, -1), ret_w_o, x, gate[0],
                                ln_g[0], ln_b[0], shift[1], scale[1],
                                heads=RET_HEADS, dk=ret_dk, dv=ret_dv)
    x2d, h2d = x1.reshape(b * s, d), h1.reshape(b * s, d)

    qk = _inproj_plain(h2d, moba_w_qk)
    vg = _inproj_plain(h2d, moba_w_in[0], col0=moba_qk)
    u = _moba_core(qk.reshape(b, s, -1), vg.reshape(b, s, -1), mct, mst,
                   heads=MOBA_HEADS, dh=moba_dh)
    x2d = _outproj_ln(u.reshape(b * s, -1), moba_w_o, x2d, gate[1],
                      ln_g[1], ln_b[1], seq=s, tm=1024)
    return x2d.reshape(b, s, d)
```
